```python
import math
import jax, jax.numpy as jnp
from jax import lax
import numpy as np

D_MODEL = 1024
BATCH = 8
SEQ = 4096
DEPTH = 1

CHUNK = 64
D_MIX = D_MODEL
D_LRU = D_MIX // 2
D_SGU = D_MIX - D_LRU
LRU_HEADS = 8
LRU_HEAD_DIM = D_LRU // LRU_HEADS
CONV_WIDTH = 4
LRU_C = 8.0
SGU_HEADS = 8
SGU_HEAD_DIM = D_SGU // SGU_HEADS
SGU_BLOCK = 128
N_GROUPS = 4
EXPERTS_PER_GROUP = 8
N_EXPERTS = N_GROUPS * EXPERTS_PER_GROUP
TOP_K = 2
D_EXPERT = D_MODEL // 2
MOE_BLOCK = 512
EPS = 1e-6

kernel_name = "hybrid_rglru_gmlp_hmoe_block"


def rmsnorm(x, g):
    xf = x.astype(jnp.float32)
    y = xf * lax.rsqrt(jnp.mean(xf * xf, axis=-1, keepdims=True) + EPS)
    return (y * g.astype(jnp.float32)).astype(x.dtype)


def layernorm(x, g, b):
    xf = x.astype(jnp.float32)
    mu = jnp.mean(xf, axis=-1, keepdims=True)
    var = jnp.mean(jnp.square(xf - mu), axis=-1, keepdims=True)
    y = (xf - mu) * lax.rsqrt(var + EPS) * g.astype(jnp.float32) + b.astype(jnp.float32)
    return y.astype(x.dtype)


def causal_conv(x, w, b):
    y = lax.conv_general_dilated(
        x, w[:, None, :].astype(x.dtype), window_strides=(1,),
        padding=[(CONV_WIDTH - 1, 0)], dimension_numbers=('NWC', 'WIO', 'NWC'),
        feature_group_count=x.shape[-1])
    return y + b.astype(x.dtype)


def rg_lru(x, w_a, b_a, w_i, b_i, lam):
    B, S, _ = x.shape
    xf = x.astype(jnp.float32)
    xh = xf.reshape(B, S, LRU_HEADS, LRU_HEAD_DIM)
    r = jax.nn.sigmoid(jnp.einsum('bshd,hde->bshe', xh, w_a.astype(jnp.float32)).reshape(B, S, D_LRU)
                       + b_a.astype(jnp.float32))
    i = jax.nn.sigmoid(jnp.einsum('bshd,hde->bshe', xh, w_i.astype(jnp.float32)).reshape(B, S, D_LRU)
                       + b_i.astype(jnp.float32))
    log_a = -LRU_C * r * jax.nn.softplus(-lam.astype(jnp.float32))
    a = jnp.exp(log_a)
    u = jnp.sqrt(-jnp.expm1(2.0 * log_a)) * (i * xf)

    def combine(lhs, rhs):
        a1, b1 = lhs
        a2, b2 = rhs
        return a1 * a2, a2 * b1 + b2

    _, h = lax.associative_scan(combine, (a, u), axis=1)
    return h.astype(x.dtype)


def spatial_gating(u, v, ln_g, ln_b, w_s, b_s):
    B, S, _ = v.shape
    pos_chunk = jnp.arange(SGU_BLOCK) // CHUNK
    mask = (pos_chunk[None, :] <= pos_chunk[:, None]).astype(w_s.dtype)
    vn = layernorm(v, ln_g, ln_b)
    vb = vn.reshape(B, S // SGU_BLOCK, SGU_BLOCK, SGU_HEADS, SGU_HEAD_DIM)
    s = jnp.einsum('gij,bnjgc->bnigc', (w_s * mask[None]).astype(v.dtype), vb)
    s = s + b_s.T.astype(v.dtype)[None, None, :, :, None]
    return u * s.reshape(B, S, D_SGU)


def hierarchical_moe(h, w_grp, b_grp, w_exp, b_exp, w1, w3, w2):
    B, S, D = h.shape
    T = B * S
    xt = h.reshape(T, D)
    g_logits = (xt @ w_grp).astype(jnp.float32) + b_grp.astype(jnp.float32)
    g_prob = jax.nn.softmax(g_logits, axis=-1)
    g_sel = jnp.argmax(g_logits, axis=-1)
    p_g = jnp.take_along_axis(g_prob, g_sel[:, None], axis=1)[:, 0]
    e_logits = ((xt @ w_exp).astype(jnp.float32) + b_exp.astype(jnp.float32)).reshape(T, N_GROUPS, EXPERTS_PER_GROUP)
    e_logits = jnp.take_along_axis(e_logits, g_sel[:, None, None], axis=1)[:, 0]
    e_prob = jax.nn.softmax(e_logits, axis=-1)
    top_p, top_i = lax.top_k(e_prob, TOP_K)
    top_p = top_p / jnp.sum(top_p, axis=-1, keepdims=True)
    weights = (p_g[:, None] * top_p).reshape(-1)
    expert = (g_sel[:, None] * EXPERTS_PER_GROUP + top_i).reshape(-1).astype(jnp.int32)
    token = jnp.repeat(jnp.arange(T, dtype=jnp.int32), TOP_K)

    n_slots = T * TOP_K
    cap = (n_slots + MOE_BLOCK - 1) // MOE_BLOCK * MOE_BLOCK + N_EXPERTS * MOE_BLOCK
    n_blocks = cap // MOE_BLOCK
    order = jnp.argsort(expert)
    e_sorted = expert[order]
    counts = jnp.bincount(expert, length=N_EXPERTS)
    padded = (counts + MOE_BLOCK - 1) // MOE_BLOCK * MOE_BLOCK
    start = jnp.cumsum(counts) - counts
    pend = jnp.cumsum(padded)
    pstart = pend - padded
    dest = pstart[e_sorted] + (jnp.arange(n_slots, dtype=jnp.int32) - start[e_sorted])
    tok_buf = jnp.full((cap,), T, jnp.int32).at[dest].set(token[order])
    w_buf = jnp.zeros((cap,), jnp.float32).at[dest].set(weights[order])
    block_expert = jnp.minimum(
        jnp.searchsorted(pend, jnp.arange(n_blocks, dtype=jnp.int32) * MOE_BLOCK, side='right'),
        N_EXPERTS - 1).astype(jnp.int32)

    x_pad = jnp.concatenate([xt, jnp.zeros((1, D), xt.dtype)], axis=0)
    xs = x_pad[tok_buf].reshape(n_blocks, MOE_BLOCK, D)

    def expert_block(args):
        xb, e = args
        return (jax.nn.silu(xb @ w1[e]) * (xb @ w3[e])) @ w2[e]

    ys = lax.map(expert_block, (xs, block_expert)).reshape(cap, D)
    ys = ys * w_buf[:, None].astype(ys.dtype)
    out = jnp.zeros((T + 1, D), ys.dtype).at[tok_buf].add(ys)[:T]
    return out.reshape(B, S, D)


def setup_inputs(seed: int = 0) -> dict:
    key = jax.random.key(seed)
    ks = jax.random.split(key, 32)
    f32 = jnp.float32
    L = DEPTH
    nrm = lambda k, shape, scale: jax.random.normal(k, shape, f32) * scale
    gain = lambda k, shape: 1.0 + 0.01 * jax.random.normal(k, shape, f32)
    base = jax.random.uniform(ks[10], (L, D_LRU), f32, 0.9, 0.999) ** (1.0 / LRU_C)
    lru_lambda = jnp.log(base) - jnp.log1p(-base)
    return {
        "x": nrm(ks[0], (BATCH, SEQ, D_MODEL), 1.0),
        "c": nrm(ks[1], (BATCH, D_MODEL), 1.0),
        "ada_w": nrm(ks[2], (L, D_MODEL, 6 * D_MODEL), D_MODEL ** -0.5),
        "ada_b": nrm(ks[3], (L, 6 * D_MODEL), 0.01),
        "norm1_g": gain(ks[4], (L, D_MODEL)),
        "w_in": nrm(ks[5], (L, D_MODEL, 2 * D_MIX), D_MODEL ** -0.5),
        "conv_w": nrm(ks[6], (L, CONV_WIDTH, D_LRU), CONV_WIDTH ** -0.5),
        "conv_b": nrm(ks[7], (L, D_LRU), 0.01),
        "gate_a_w": nrm(ks[8], (L, LRU_HEADS, LRU_HEAD_DIM, LRU_HEAD_DIM), LRU_HEAD_DIM ** -0.5),
        "gate_a_b": nrm(ks[9], (L, D_LRU), 0.01),
        "gate_i_w": nrm(ks[11], (L, LRU_HEADS, LRU_HEAD_DIM, LRU_HEAD_DIM), LRU_HEAD_DIM ** -0.5),
        "gate_i_b": nrm(ks[12], (L, D_LRU), 0.01),
        "lru_lambda": lru_lambda,
        "sgu_ln_g": gain(ks[13], (L, D_SGU)),
        "sgu_ln_b": nrm(ks[14], (L, D_SGU), 0.01),
        "sgu_w": nrm(ks[15], (L, SGU_HEADS, SGU_BLOCK, SGU_BLOCK), SGU_BLOCK ** -0.5),
        "sgu_b": gain(ks[16], (L, SGU_HEADS, SGU_BLOCK)),
        "w_out": nrm(ks[17], (L, D_MIX, D_MODEL), D_MIX ** -0.5),
        "norm2_g": gain(ks[18], (L, D_MODEL)),
        "router_group_w": nrm(ks[19], (L, D_MODEL, N_GROUPS), D_MODEL ** -0.5),
        "router_group_b": nrm(ks[20], (L, N_GROUPS), 0.01),
        "router_expert_w": nrm(ks[21], (L, D_MODEL, N_EXPERTS), D_MODEL ** -0.5),
        "router_expert_b": nrm(ks[22], (L, N_EXPERTS), 0.01),
        "expert_w1": nrm(ks[23], (L, N_EXPERTS, D_MODEL, D_EXPERT), D_MODEL ** -0.5),
        "expert_w3": nrm(ks[24], (L, N_EXPERTS, D_MODEL, D_EXPERT), D_MODEL ** -0.5),
        "expert_w2": nrm(ks[25], (L, N_EXPERTS, D_EXPERT, D_MODEL), D_EXPERT ** -0.5),
        "final_g": gain(ks[26], (D_MODEL,)),
    }


def reference(x, c, ada_w, ada_b, norm1_g, w_in, conv_w, conv_b, gate_a_w, gate_a_b, gate_i_w,
              gate_i_b, lru_lambda, sgu_ln_g, sgu_ln_b, sgu_w, sgu_b, w_out, norm2_g,
              router_group_w, router_group_b, router_expert_w, router_expert_b,
              expert_w1, expert_w3, expert_w2, final_g):
    cond = jax.nn.silu(c)
    for l in range(DEPTH):
        mod = cond @ ada_w[l] + ada_b[l]
        sh1, sc1, g1, sh2, sc2, g2 = jnp.split(mod, 6, axis=-1)
        h = rmsnorm(x, norm1_g[l]) * (1.0 + sc1[:, None]) + sh1[:, None]
        z = h @ w_in[l]
        xr, gr, u, v = jnp.split(z, [D_LRU, 2 * D_LRU, 2 * D_LRU + D_SGU], axis=-1)
        xr = causal_conv(xr, conv_w[l], conv_b[l])
        y_lru = rg_lru(xr, gate_a_w[l], gate_a_b[l], gate_i_w[l], gate_i_b[l], lru_lambda[l]) * jax.nn.gelu(gr)
        y_sgu = spatial_gating(jax.nn.gelu(u), jax.nn.gelu(v), sgu_ln_g[l], sgu_ln_b[l], sgu_w[l], sgu_b[l])
        mix = jnp.concatenate([y_lru, y_sgu], axis=-1) @ w_out[l]
        x = x + g1[:, None] * mix
        h = rmsnorm(x, norm2_g[l]) * (1.0 + sc2[:, None]) + sh2[:, None]
        y = hierarchical_moe(h, router_group_w[l], router_group_b[l], router_expert_w[l], router_expert_b[l],
                             expert_w1[l], expert_w3[l], expert_w2[l])
        x = x + g2[:, None] * y
    return rmsnorm(x, final_g)
```

```python
import functools

import jax
import jax.numpy as jnp
from jax import lax
from jax.experimental import pallas as pl
from jax.experimental.pallas import tpu as pltpu

F32 = jnp.float32
BF16 = jnp.bfloat16

D = 1024
B = 8
S = 4096
T = B * S
D_LRU = 512
D_SGU = 512
HEADS = 8
HEAD_DIM = 64
CONV_WIDTH = 4
LRU_C = 8.0
SGU_BLOCK = 128
CHUNK = 64
N_GROUPS = 4
EPG = 8
N_EXPERTS = N_GROUPS * EPG
TOP_K = 2
D_EXPERT = 512
EPS = 1e-6

LANES = 128
SUBLANES = 8
VMEM_LIMIT = 60 * 1024 * 1024

LT = SGU_BLOCK
GB = 4
NGRP = B // GB
GROWS = GB * LT
PITCH = LT + SUBLANES
HALO = SUBLANES
NSLAB = D_LRU // LANES

TM = 512
N_SLOTS = T * TOP_K
CAP = N_SLOTS + N_EXPERTS * TM
NBLK = CAP // TM
NBLK_PAD = 256
TR = 1024
TKD = 512
TKC = 256
LOGIT_W = LANES


def _gelu(x):
    return 0.5 * x * (1.0 + jnp.tanh(0.7978845608028654 * (x + 0.044715 * (x * x * x))))


def _sigmoid(x):
    return 1.0 / (1.0 + jnp.exp(-x))


def _mod_kernel(c_ref, w_ref, b_ref, mul_ref, add_ref, o_ref):
    c = c_ref[...]
    cond = c * _sigmoid(c)
    m = jnp.dot(cond, w_ref[...], preferred_element_type=F32, precision=lax.Precision.HIGHEST)
    m = m + b_ref[0]
    o_ref[0] = m * mul_ref[0] + add_ref[0]


def _mod_call(c, ada_w, ada_b, mul, add):
    return pl.pallas_call(
        _mod_kernel,
        grid=(6,),
        in_specs=[
            pl.BlockSpec((B, D), lambda k: (0, 0)),
            pl.BlockSpec((D, D), lambda k: (0, k)),
            pl.BlockSpec((1, 1, D), lambda k: (k, 0, 0)),
            pl.BlockSpec((1, 1, D), lambda k: (k, 0, 0)),
            pl.BlockSpec((1, 1, D), lambda k: (k, 0, 0)),
        ],
        out_specs=pl.BlockSpec((1, B, D), lambda k: (k, 0, 0)),
        out_shape=jax.ShapeDtypeStruct((6, B, D), F32),
        name="mod",
    )(c, ada_w, ada_b.reshape(6, 1, D), mul, add)


def _mixer_kernel(x_ref, mod_ref, win_ref, convw_ref, convb_ref, wg_ref, bg_ref, lam_ref,
                  lng_ref, lnb_ref, sguw_ref, sgub_ref, wout_ref, wr_ref, br_ref,
                  x1_ref, h2_ref, lg_ref,
                  h_scr, xr_scr, a_scr, u_scr, hc_scr, y_scr):
    t = pl.program_id(0)

    @pl.when(t == 0)
    def _init():
        hc_scr[...] = jnp.zeros_like(hc_scr)
        xr_scr[:, 0:HALO, :] = jnp.zeros((B, HALO, D_LRU), F32)

    lam = lam_ref[...]
    z = -lam
    softplus = jnp.maximum(z, 0.0) + jnp.log1p(jnp.exp(-jnp.abs(z)))
    clam = -LRU_C * softplus
    cw = convw_ref[...]
    cb = convb_ref[...]

    for grp in range(NGRP):
        for j in range(GB):
            b = grp * GB + j
            xb = x_ref[b]
            ms = jnp.mean(xb * xb, axis=-1, keepdims=True)
            hb = (xb * lax.rsqrt(ms + EPS)) * mod_ref[1, b:b + 1, :] + mod_ref[0, b:b + 1, :]
            h_scr[b * LT:(b + 1) * LT, :] = hb.astype(BF16)
        hg = h_scr[grp * GROWS:(grp + 1) * GROWS, :]
        xr = jnp.dot(hg, win_ref[:, 0:D_LRU], preferred_element_type=F32)
        xcs = []
        for j in range(GB):
            b = grp * GB + j
            xr_scr[b, HALO:HALO + LT, :] = xr[j * LT:(j + 1) * LT, :]
            acc = cb + cw[CONV_WIDTH - 1:CONV_WIDTH] * xr_scr[b, HALO:HALO + LT, :]
            for k in range(CONV_WIDTH - 1):
                off = HALO - (CONV_WIDTH - 1) + k
                acc = acc + cw[k:k + 1] * xr_scr[b, off:off + LT, :]
            xcs.append(acc)
            xr_scr[b, 0:HALO, :] = xr_scr[b, LT:LT + HALO, :]
        xc = jnp.concatenate(xcs, axis=0)
        gpre = jnp.dot(xc.astype(BF16), wg_ref[...], preferred_element_type=F32) + bg_ref[...]
        r = _sigmoid(gpre[:, 0:D_LRU])
        gi = _sigmoid(gpre[:, D_LRU:2 * D_LRU])
        a = jnp.exp(clam * r)
        uin = jnp.sqrt(1.0 - a * a) * (gi * xc)
        for j in range(GB):
            b = grp * GB + j
            for s in range(NSLAB):
                a_scr[s, b * PITCH:b * PITCH + LT, :] = a[j * LT:(j + 1) * LT, s * LANES:(s + 1) * LANES]
                u_scr[s, b * PITCH:b * PITCH + LT, :] = uin[j * LT:(j + 1) * LT, s * LANES:(s + 1) * LANES]

    def scan_body(tt, carry):
        out = []
        for s in range(NSLAB):
            at = a_scr[s, pl.ds(tt, B, stride=PITCH), :]
            ut = u_scr[s, pl.ds(tt, B, stride=PITCH), :]
            hn = at * carry[s] + ut
            u_scr[s, pl.ds(tt, B, stride=PITCH), :] = hn
            out.append(hn)
        return tuple(out)

    carry = lax.fori_loop(0, LT, scan_body, tuple(hc_scr[s] for s in range(NSLAB)), unroll=8)
    for s in range(NSLAB):
        hc_scr[s] = carry[s]

    for grp in range(NGRP):
        hg = h_scr[grp * GROWS:(grp + 1) * GROWS, :]
        gr = jnp.dot(hg, win_ref[:, D_LRU:2 * D_LRU], preferred_element_type=F32)
        gg = _gelu(gr)
        for j in range(GB):
            b = grp * GB + j
            hl = jnp.concatenate([u_scr[s, b * PITCH:b * PITCH + LT, :] for s in range(NSLAB)], axis=1)
            y_scr[j * LT:(j + 1) * LT, 0:D_LRU] = (hl * gg[j * LT:(j + 1) * LT, :]).astype(BF16)

        su = jnp.dot(hg, win_ref[:, 2 * D_LRU:2 * D_LRU + D_SGU], preferred_element_type=F32)
        sv = jnp.dot(hg, win_ref[:, 2 * D_LRU + D_SGU:], preferred_element_type=F32)
        ug = _gelu(su)
        vg = _gelu(sv)
        mu = jnp.mean(vg, axis=-1, keepdims=True)
        vc = vg - mu
        var = jnp.mean(vc * vc, axis=-1, keepdims=True)
        vn = vc * lax.rsqrt(var + EPS) * lng_ref[...] + lnb_ref[...]
        vnt = [vn[j * LT:(j + 1) * LT, :].T for j in range(GB)]
        st = []
        for g in range(HEADS):
            lhs = jnp.concatenate([vnt[j][g * HEAD_DIM:(g + 1) * HEAD_DIM, :] for j in range(GB)], axis=0)
            st.append(jnp.dot(lhs.astype(BF16), sguw_ref[g], preferred_element_type=F32))
        for j in range(GB):
            stj = jnp.concatenate([st[g][j * HEAD_DIM:(j + 1) * HEAD_DIM, :] for g in range(HEADS)], axis=0)
            sj = stj.T + sgub_ref[...]
            y_scr[j * LT:(j + 1) * LT, D_LRU:] = (ug[j * LT:(j + 1) * LT, :] * sj).astype(BF16)

        mix = jnp.dot(y_scr[...], wout_ref[...], preferred_element_type=F32)
        for j in range(GB):
            b = grp * GB + j
            x1 = x_ref[b] + mod_ref[2, b:b + 1, :] * mix[j * LT:(j + 1) * LT, :]
            x1_ref[b] = x1
            ms = jnp.mean(x1 * x1, axis=-1, keepdims=True)
            h2 = (x1 * lax.rsqrt(ms + EPS)) * mod_ref[4, b:b + 1, :] + mod_ref[3, b:b + 1, :]
            h2_ref[b] = h2
            h_scr[b * LT:(b + 1) * LT, :] = h2.astype(BF16)
        lg = jnp.dot(h_scr[grp * GROWS:(grp + 1) * GROWS, :], wr_ref[...],
                     preferred_element_type=F32) + br_ref[...]
        for j in range(GB):
            b = grp * GB + j
            lg_ref[b] = lg[j * LT:(j + 1) * LT, :]


def _const_spec(shape):
    nd = len(shape)
    return pl.BlockSpec(shape, lambda t, _nd=nd: (0,) * _nd, pipeline_mode=pl.Buffered(1))


def _mixer_call(x, mod, win, convw, convb, wg, bg, lam, lng, lnb, sguw, sgub, wout, wr, br):
    tile = lambda w: pl.BlockSpec((B, LT, w), lambda t: (0, t, 0))
    return pl.pallas_call(
        _mixer_kernel,
        grid=(S // LT,),
        in_specs=[
            tile(D),
            _const_spec((6, B, D)),
            _const_spec((D, 2 * D)),
            _const_spec((CONV_WIDTH, D_LRU)),
            _const_spec((1, D_LRU)),
            _const_spec((D_LRU, 2 * D_LRU)),
            _const_spec((1, 2 * D_LRU)),
            _const_spec((1, D_LRU)),
            _const_spec((1, D_SGU)),
            _const_spec((1, D_SGU)),
            _const_spec((HEADS, SGU_BLOCK, SGU_BLOCK)),
            _const_spec((SGU_BLOCK, D_SGU)),
            _const_spec((D, D)),
            _const_spec((D, LOGIT_W)),
            _const_spec((1, LOGIT_W)),
        ],
        out_specs=[tile(D), tile(D), tile(LOGIT_W)],
        out_shape=[
            jax.ShapeDtypeStruct((B, S, D), F32),
            jax.ShapeDtypeStruct((B, S, D), F32),
            jax.ShapeDtypeStruct((B, S, LOGIT_W), F32),
        ],
        scratch_shapes=[
            pltpu.VMEM((B * LT, D), BF16),
            pltpu.VMEM((B, LT + HALO, D_LRU), F32),
            pltpu.VMEM((NSLAB, B * PITCH, LANES), F32),
            pltpu.VMEM((NSLAB, B * PITCH, LANES), F32),
            pltpu.VMEM((NSLAB, B, LANES), F32),
            pltpu.VMEM((GROWS, D), BF16),
        ],
        compiler_params=pltpu.CompilerParams(
            dimension_semantics=("arbitrary",), vmem_limit_bytes=VMEM_LIMIT),
        name="mixer",
    )(x, mod, win, convw, convb, wg, bg, lam, lng, lnb, sguw, sgub, wout, wr, br)


def _route_kernel(lg_ref, dest_ref, wt_ref, be_ref, cnt_scr, pst_scr, run_scr):
    p = pl.program_id(0)
    i = pl.program_id(1)

    lt = lg_ref[...].T
    row8 = lax.broadcasted_iota(jnp.int32, (SUBLANES, TR), 0)
    gl = jnp.where(row8 < N_GROUPS, lt[0:SUBLANES], -jnp.inf)
    gmax = jnp.max(gl, axis=0, keepdims=True)
    gsel = jnp.min(jnp.where(gl == gmax, row8, SUBLANES), axis=0, keepdims=True)
    pg = 1.0 / jnp.sum(jnp.exp(gl - gmax), axis=0, keepdims=True)
    el = lt[SUBLANES:2 * SUBLANES]
    for g in range(1, N_GROUPS):
        el = jnp.where(gsel == g, lt[(g + 1) * SUBLANES:(g + 2) * SUBLANES], el)
    emax = jnp.max(el, axis=0, keepdims=True)
    pe = jnp.exp(el - emax)
    prob = pe / jnp.sum(pe, axis=0, keepdims=True)
    p0 = jnp.max(prob, axis=0, keepdims=True)
    i0 = jnp.min(jnp.where(prob == p0, row8, SUBLANES), axis=0, keepdims=True)
    pm = jnp.where(row8 == i0, -1.0, prob)
    p1 = jnp.max(pm, axis=0, keepdims=True)
    i1 = jnp.min(jnp.where(pm == p1, row8, SUBLANES), axis=0, keepdims=True)
    den = p0 + p1
    w0 = pg * (p0 / den)
    w1 = pg * (p1 / den)
    e0 = gsel * EPG + i0
    e1 = gsel * EPG + i1

    row32 = lax.broadcasted_iota(jnp.int32, (N_EXPERTS, TR), 0)
    is0 = row32 == e0
    is1 = row32 == e1
    oh = jnp.where(is0, 1.0, 0.0) + jnp.where(is1, 1.0, 0.0)
    ohsum = jnp.sum(oh, axis=1, keepdims=True)

    @pl.when((p == 0) & (i == 0))
    def _zero():
        cnt_scr[...] = jnp.zeros_like(cnt_scr)

    @pl.when(p == 0)
    def _count():
        cnt_scr[...] = cnt_scr[...] + ohsum

    @pl.when((p == 1) & (i == 0))
    def _starts():
        cnt = cnt_scr[...]
        ntile = jnp.ceil(cnt * (1.0 / TM))
        r = lax.broadcasted_iota(jnp.int32, (N_EXPERTS, N_EXPERTS), 0)
        c = lax.broadcasted_iota(jnp.int32, (N_EXPERTS, N_EXPERTS), 1)
        lower = jnp.where(c < r, 1.0, 0.0).astype(BF16)
        tstart = jnp.dot(lower, ntile.astype(BF16), preferred_element_type=F32)
        pst_scr[...] = tstart * TM
        run_scr[...] = jnp.zeros_like(run_scr)
        tend = tstart + ntile
        lane = lax.broadcasted_iota(jnp.int32, (N_EXPERTS, NBLK_PAD), 1).astype(F32)
        be = jnp.sum(jnp.where(tend[:, 0:1] <= lane, 1.0, 0.0), axis=0, keepdims=True)
        be = jnp.minimum(be, N_EXPERTS - 1.0)
        nused = jnp.sum(ntile[:, 0:1], axis=0, keepdims=True)
        rowb = lax.broadcasted_iota(jnp.int32, (SUBLANES, NBLK_PAD), 0)
        be_ref[...] = jnp.where(rowb == 0, be, nused).astype(jnp.int32)

    @pl.when(p == 1)
    def _dest():
        rr = lax.broadcasted_iota(jnp.int32, (TR, TR), 0)
        cc = lax.broadcasted_iota(jnp.int32, (TR, TR), 1)
        upper = jnp.where(rr < cc, 1.0, 0.0).astype(BF16)
        cum = jnp.dot(oh.astype(BF16), upper, preferred_element_type=F32)
        pos = cum + (pst_scr[:, 0:1] + run_scr[:, 0:1])
        d0 = jnp.sum(jnp.where(is0, pos, 0.0), axis=0, keepdims=True)
        d1 = jnp.sum(jnp.where(is1, pos, 0.0), axis=0, keepdims=True)
        run_scr[...] = run_scr[...] + ohsum
        dest_ref[...] = jnp.where(row8 == 0, d0, jnp.where(row8 == 1, d1, 0.0)).astype(jnp.int32)
        wt_ref[...] = jnp.where(row8 == 0, w0, jnp.where(row8 == 1, w1, 0.0))


def _route_call(logits):
    return pl.pallas_call(
        _route_kernel,
        grid=(2, T // TR),
        in_specs=[pl.BlockSpec((TR, LOGIT_W), lambda p, i: (i, 0))],
        out_specs=[
            pl.BlockSpec((SUBLANES, TR), lambda p, i: (0, i * p)),
            pl.BlockSpec((SUBLANES, TR), lambda p, i: (0, i * p)),
            pl.BlockSpec((SUBLANES, NBLK_PAD), lambda p, i: (0, 0)),
        ],
        out_shape=[
            jax.ShapeDtypeStruct((SUBLANES, T), jnp.int32),
            jax.ShapeDtypeStruct((SUBLANES, T), F32),
            jax.ShapeDtypeStruct((SUBLANES, NBLK_PAD), jnp.int32),
        ],
        scratch_shapes=[
            pltpu.VMEM((N_EXPERTS, LANES), F32),
            pltpu.VMEM((N_EXPERTS, LANES), F32),
            pltpu.VMEM((N_EXPERTS, LANES), F32),
        ],
        compiler_params=pltpu.CompilerParams(
            dimension_semantics=("arbitrary", "arbitrary"), vmem_limit_bytes=VMEM_LIMIT),
        name="route",
    )(logits)


def _row_copy(src, src_row, dst, dst_row, sem):
    return pltpu.make_async_copy(src.at[pl.ds(src_row, 1)], dst.at[pl.ds(dst_row, 1)], sem)


def _dispatch_kernel(dest_ref, h2_ref, xs_in_ref, xs_ref, sem):
    del xs_in_ref
    i = pl.program_id(0)
    n = pl.num_programs(0)

    def start(r, carry):
        tok = i * TKD + r
        _row_copy(h2_ref, tok, xs_ref, dest_ref[0, r], sem).start()
        _row_copy(h2_ref, tok, xs_ref, dest_ref[0, TKD + r], sem).start()
        return carry

    lax.fori_loop(0, TKD, start, 0, unroll=8)

    def wait(r, carry):
        _row_copy(h2_ref, 0, xs_ref, 0, sem).wait()
        _row_copy(h2_ref, 0, xs_ref, 0, sem).wait()
        return carry

    @pl.when(i > 0)
    def _():
        lax.fori_loop(0, TKD, wait, 0)

    @pl.when(i == n - 1)
    def _():
        lax.fori_loop(0, TKD, wait, 0)


def _dispatch_call(dest_tiles, h2, xs_zero):
    return pl.pallas_call(
        _dispatch_kernel,
        grid=(T // TKD,),
        in_specs=[
            pl.BlockSpec((None, 1, 2 * TKD), lambda i: (i, 0, 0), memory_space=pltpu.SMEM),
            pl.BlockSpec(memory_space=pl.ANY),
            pl.BlockSpec(memory_space=pl.ANY),
        ],
        out_specs=pl.BlockSpec(memory_space=pl.ANY),
        out_shape=jax.ShapeDtypeStruct((CAP, D), F32),
        scratch_shapes=[pltpu.SemaphoreType.DMA(())],
        input_output_aliases={2: 0},
        compiler_params=pltpu.CompilerParams(dimension_semantics=("arbitrary",)),
        name="dispatch",
    )(dest_tiles, h2, xs_zero)


def _experts_kernel(be_ref, nu_ref, xs_ref, w1_ref, w3_ref, w2_ref, ys_ref):
    j = pl.program_id(0)

    @pl.when(j < nu_ref[0])
    def _active():
        xb = xs_ref[...].astype(BF16)
        h1 = jnp.dot(xb, w1_ref[0], preferred_element_type=F32)
        h3 = jnp.dot(xb, w3_ref[0], preferred_element_type=F32)
        act = (h1 * _sigmoid(h1)) * h3
        ys_ref[...] = jnp.dot(act.astype(BF16), w2_ref[0], preferred_element_type=F32)

    @pl.when(j >= nu_ref[0])
    def _idle():
        ys_ref[...] = jnp.zeros_like(ys_ref)


def _experts_call(block_expert, n_used, xs, w1, w3, w2):
    def x_map(j, be, nu):
        return (jnp.minimum(j, nu[0] - 1), 0)

    def w_map(j, be, nu):
        return (be[j], 0, 0)

    grid_spec = pltpu.PrefetchScalarGridSpec(
        num_scalar_prefetch=2,
        grid=(NBLK,),
        in_specs=[
            pl.BlockSpec((TM, D), x_map),
            pl.BlockSpec((1, D, D_EXPERT), w_map),
            pl.BlockSpec((1, D, D_EXPERT), w_map),
            pl.BlockSpec((1, D_EXPERT, D), w_map),
        ],
        out_specs=pl.BlockSpec((TM, D), lambda j, be, nu: (j, 0)),
    )
    return pl.pallas_call(
        _experts_kernel,
        grid_spec=grid_spec,
        out_shape=jax.ShapeDtypeStruct((CAP, D), F32),
        compiler_params=pltpu.CompilerParams(
            dimension_semantics=("arbitrary",), vmem_limit_bytes=VMEM_LIMIT),
        name="experts",
    )(block_expert, n_used, xs, w1, w3, w2)


def _combine_kernel(dcur_ref, dnext_ref, x1_ref, g2_ref, w0_ref, w1_ref, fg_ref, ys_ref,
                    o_ref, ybuf, sems):
    i = pl.program_id(0)
    n = pl.num_programs(0)
    slot = lax.rem(i, 2)

    def gather(dref, sl):
        def body(r, carry):
            _row_copy(ys_ref, dref[0, r], ybuf.at[sl, 0], r, sems.at[sl]).start()
            _row_copy(ys_ref, dref[0, TKC + r], ybuf.at[sl, 1], r, sems.at[sl]).start()
            return carry
        lax.fori_loop(0, TKC, body, 0, unroll=8)

    @pl.when(i == 0)
    def _():
        gather(dcur_ref, 0)

    @pl.when(i + 1 < n)
    def _():
        gather(dnext_ref, 1 - slot)

    def wait(r, carry):
        _row_copy(ys_ref, 0, ybuf.at[slot, 0], 0, sems.at[slot]).wait()
        _row_copy(ys_ref, 0, ybuf.at[slot, 1], 0, sems.at[slot]).wait()
        return carry

    lax.fori_loop(0, TKC, wait, 0)

    y = w0_ref[...] * ybuf[slot, 0] + w1_ref[...] * ybuf[slot, 1]
    x2 = x1_ref[...] + g2_ref[0] * y
    ms = jnp.mean(x2 * x2, axis=-1, keepdims=True)
    o_ref[...] = (x2 * lax.rsqrt(ms + EPS)) * fg_ref[...]


def _combine_call(dest_tiles, x1, mod, w0, w1, final_g, ys):
    nt = T // TKC
    per_b = S // TKC
    return pl.pallas_call(
        _combine_kernel,
        grid=(nt,),
        in_specs=[
            pl.BlockSpec((None, 1, 2 * TKC), lambda i: (i, 0, 0), memory_space=pltpu.SMEM),
            pl.BlockSpec((None, 1, 2 * TKC), lambda i: (jnp.minimum(i + 1, nt - 1), 0, 0),
                         memory_space=pltpu.SMEM),
            pl.BlockSpec((TKC, D), lambda i: (i, 0)),
            pl.BlockSpec((None, 1, 1, D), lambda i: (5, i // per_b, 0, 0)),
            pl.BlockSpec((TKC, 1), lambda i: (i, 0)),
            pl.BlockSpec((TKC, 1), lambda i: (i, 0)),
            pl.BlockSpec((1, D), lambda i: (0, 0)),
            pl.BlockSpec(memory_space=pl.ANY),
        ],
        out_specs=pl.BlockSpec((TKC, D), lambda i: (i, 0)),
        out_shape=jax.ShapeDtypeStruct((T, D), F32),
        scratch_shapes=[
            pltpu.VMEM((2, 2, TKC, D), F32),
            pltpu.SemaphoreType.DMA((2,)),
        ],
        compiler_params=pltpu.CompilerParams(
            dimension_semantics=("arbitrary",), vmem_limit_bytes=VMEM_LIMIT),
        name="combine",
    )(dest_tiles, dest_tiles, x1, mod.reshape(6, B, 1, D), w0, w1, final_g, ys)


def _block_diag(w):
    eye = jnp.eye(HEADS, dtype=w.dtype)
    return jnp.einsum("hde,hg->hdge", w, eye).reshape(D_LRU, D_LRU)


def _dest_tiles(dest, tk):
    nt = T // tk
    return dest[0:2].reshape(2, nt, tk).transpose(1, 0, 2).reshape(nt, 1, 2 * tk)


def kernel(x, c, ada_w, ada_b, norm1_g, w_in, conv_w, conv_b, gate_a_w, gate_a_b, gate_i_w, gate_i_b, lru_lambda, sgu_ln_g, sgu_ln_b, sgu_w, sgu_b, w_out, norm2_g, router_group_w, router_group_b, router_expert_w, router_expert_b, expert_w1, expert_w3, expert_w2, final_g):
    l = 0
    ones = jnp.ones((D,), F32)
    zeros = jnp.zeros((D,), F32)
    mul = jnp.stack([ones, norm1_g[l], ones, ones, norm2_g[l], ones]).reshape(6, 1, D)
    add = jnp.stack([zeros, norm1_g[l], zeros, zeros, norm2_g[l], zeros]).reshape(6, 1, D)
    mod = _mod_call(c, ada_w[l], ada_b[l], mul, add)

    pos_chunk = jnp.arange(SGU_BLOCK) // CHUNK
    mask = (pos_chunk[None, :] <= pos_chunk[:, None]).astype(F32)
    sguw_t = jnp.transpose(sgu_w[l] * mask[None], (0, 2, 1)).astype(BF16)
    sgub_tile = jnp.repeat(sgu_b[l].T, HEAD_DIM, axis=1)
    wg = jnp.concatenate([_block_diag(gate_a_w[l]), _block_diag(gate_i_w[l])], axis=1).astype(BF16)
    bg = jnp.concatenate([gate_a_b[l], gate_i_b[l]]).reshape(1, 2 * D_LRU)
    wr = jnp.zeros((D, LOGIT_W), F32)
    wr = wr.at[:, 0:N_GROUPS].set(router_group_w[l]).at[:, SUBLANES:SUBLANES + N_EXPERTS].set(router_expert_w[l])
    br = jnp.zeros((1, LOGIT_W), F32)
    br = br.at[0, 0:N_GROUPS].set(router_group_b[l]).at[0, SUBLANES:SUBLANES + N_EXPERTS].set(router_expert_b[l])

    x1, h2, logits = _mixer_call(
        x, mod, w_in[l].astype(BF16), conv_w[l], conv_b[l].reshape(1, D_LRU), wg, bg,
        lru_lambda[l].reshape(1, D_LRU), sgu_ln_g[l].reshape(1, D_SGU), sgu_ln_b[l].reshape(1, D_SGU),
        sguw_t, sgub_tile, w_out[l].astype(BF16), wr.astype(BF16), br)

    dest, wts, be = _route_call(logits.reshape(T, LOGIT_W))
    block_expert = be[0, 0:NBLK]
    n_used = be[1, 0:1]

    h2 = h2.reshape(T, D)
    xs = _dispatch_call(_dest_tiles(dest, TKD), h2, jnp.zeros((CAP, D), F32))
    ys = _experts_call(block_expert, n_used, xs, expert_w1[l].astype(BF16), expert_w3[l].astype(BF16),
                       expert_w2[l].astype(BF16))
    out = _combine_call(_dest_tiles(dest, TKC), x1.reshape(T, D), mod, wts[0].reshape(T, 1),
                        wts[1].reshape(T, 1), final_g.reshape(1, D), ys)
    return out.reshape(B, S, D)
```

```python
import functools

import jax
import jax.numpy as jnp
from jax import lax
from jax.experimental import pallas as pl
from jax.experimental.pallas import tpu as pltpu

F32 = jnp.float32
BF16 = jnp.bfloat16

D = 1024
B = 8
S = 4096
T = B * S
D_LRU = 512
D_SGU = 512
HEADS = 8
HEAD_DIM = 64
CONV_WIDTH = 4
LRU_C = 8.0
SGU_BLOCK = 128
CHUNK = 64
N_GROUPS = 4
EPG = 8
N_EXPERTS = N_GROUPS * EPG
TOP_K = 2
D_EXPERT = 512
EPS = 1e-6

LANES = 128
SUBLANES = 8
VMEM_LIMIT = 60 * 1024 * 1024

LT = SGU_BLOCK
GB = 4
NGRP = B // GB
GROWS = GB * LT
PITCH = LT + SUBLANES
HALO = SUBLANES
NSLAB = D_LRU // LANES

TM = 512
N_SLOTS = T * TOP_K
CAP = N_SLOTS + N_EXPERTS * TM
NBLK = CAP // TM
NBLK_PAD = 256
TR = 1024
TKD = 1024
TKC = 256
LOGIT_W = LANES


def _gelu(x):
    return 0.5 * x * (1.0 + jnp.tanh(0.7978845608028654 * (x + 0.044715 * (x * x * x))))


def _sigmoid(x):
    return 1.0 / (1.0 + jnp.exp(-x))


def _mod_kernel(c_ref, w_ref, b_ref, mul_ref, add_ref, o_ref):
    c = c_ref[...]
    cond = c * _sigmoid(c)
    m = jnp.dot(cond, w_ref[...], preferred_element_type=F32, precision=lax.Precision.HIGHEST)
    m = m + b_ref[0]
    o_ref[0] = m * mul_ref[0] + add_ref[0]


def _mod_call(c, ada_w, ada_b, mul, add):
    return pl.pallas_call(
        _mod_kernel,
        grid=(6,),
        in_specs=[
            pl.BlockSpec((B, D), lambda k: (0, 0)),
            pl.BlockSpec((D, D), lambda k: (0, k)),
            pl.BlockSpec((1, 1, D), lambda k: (k, 0, 0)),
            pl.BlockSpec((1, 1, D), lambda k: (k, 0, 0)),
            pl.BlockSpec((1, 1, D), lambda k: (k, 0, 0)),
        ],
        out_specs=pl.BlockSpec((1, B, D), lambda k: (k, 0, 0)),
        out_shape=jax.ShapeDtypeStruct((6, B, D), F32),
        name="mod",
    )(c, ada_w, ada_b.reshape(6, 1, D), mul, add)


def _mixer_kernel(x_ref, mod_ref, win_ref, convw_ref, convb_ref, wg_ref, bg_ref, lam_ref,
                  lng_ref, lnb_ref, sguw_ref, sgub_ref, wout_ref, wr_ref, br_ref,
                  x1_ref, h2_ref, lg_ref,
                  h_scr, xr_scr, a_scr, u_scr, hc_scr, y_scr):
    t = pl.program_id(0)

    @pl.when(t == 0)
    def _init():
        hc_scr[...] = jnp.zeros_like(hc_scr)
        xr_scr[:, 0:HALO, :] = jnp.zeros((B, HALO, D_LRU), F32)

    lam = lam_ref[...]
    z = -lam
    softplus = jnp.maximum(z, 0.0) + jnp.log1p(jnp.exp(-jnp.abs(z)))
    clam = -LRU_C * softplus
    cw = convw_ref[...]
    cb = convb_ref[...]

    for grp in range(NGRP):
        for j in range(GB):
            b = grp * GB + j
            xb = x_ref[b]
            ms = jnp.mean(xb * xb, axis=-1, keepdims=True)
            hb = (xb * lax.rsqrt(ms + EPS)) * mod_ref[1, b:b + 1, :] + mod_ref[0, b:b + 1, :]
            h_scr[b * LT:(b + 1) * LT, :] = hb.astype(BF16)
        hg = h_scr[grp * GROWS:(grp + 1) * GROWS, :]
        xr = jnp.dot(hg, win_ref[:, 0:D_LRU], preferred_element_type=F32)
        xcs = []
        for j in range(GB):
            b = grp * GB + j
            xr_scr[b, HALO:HALO + LT, :] = xr[j * LT:(j + 1) * LT, :]
            acc = cb + cw[CONV_WIDTH - 1:CONV_WIDTH] * xr_scr[b, HALO:HALO + LT, :]
            for k in range(CONV_WIDTH - 1):
                off = HALO - (CONV_WIDTH - 1) + k
                acc = acc + cw[k:k + 1] * xr_scr[b, off:off + LT, :]
            xcs.append(acc)
            xr_scr[b, 0:HALO, :] = xr_scr[b, LT:LT + HALO, :]
        xc = jnp.concatenate(xcs, axis=0)
        gpre = jnp.dot(xc.astype(BF16), wg_ref[...], preferred_element_type=F32) + bg_ref[...]
        r = _sigmoid(gpre[:, 0:D_LRU])
        gi = _sigmoid(gpre[:, D_LRU:2 * D_LRU])
        a = jnp.exp(clam * r)
        uin = jnp.sqrt(1.0 - a * a) * (gi * xc)
        for j in range(GB):
            b = grp * GB + j
            for s in range(NSLAB):
                a_scr[s, b * PITCH:b * PITCH + LT, :] = a[j * LT:(j + 1) * LT, s * LANES:(s + 1) * LANES]
                u_scr[s, b * PITCH:b * PITCH + LT, :] = uin[j * LT:(j + 1) * LT, s * LANES:(s + 1) * LANES]

    def scan_body(tt, carry):
        out = []
        for s in range(NSLAB):
            at = a_scr[s, pl.ds(tt, B, stride=PITCH), :]
            ut = u_scr[s, pl.ds(tt, B, stride=PITCH), :]
            hn = at * carry[s] + ut
            u_scr[s, pl.ds(tt, B, stride=PITCH), :] = hn
            out.append(hn)
        return tuple(out)

    carry = lax.fori_loop(0, LT, scan_body, tuple(hc_scr[s] for s in range(NSLAB)), unroll=8)
    for s in range(NSLAB):
        hc_scr[s] = carry[s]

    for grp in range(NGRP):
        hg = h_scr[grp * GROWS:(grp + 1) * GROWS, :]
        gr = jnp.dot(hg, win_ref[:, D_LRU:2 * D_LRU], preferred_element_type=F32)
        gg = _gelu(gr)
        for j in range(GB):
            b = grp * GB + j
            hl = jnp.concatenate([u_scr[s, b * PITCH:b * PITCH + LT, :] for s in range(NSLAB)], axis=1)
            y_scr[j * LT:(j + 1) * LT, 0:D_LRU] = (hl * gg[j * LT:(j + 1) * LT, :]).astype(BF16)

        su = jnp.dot(hg, win_ref[:, 2 * D_LRU:2 * D_LRU + D_SGU], preferred_element_type=F32)
        sv = jnp.dot(hg, win_ref[:, 2 * D_LRU + D_SGU:], preferred_element_type=F32)
        ug = _gelu(su)
        vg = _gelu(sv)
        mu = jnp.mean(vg, axis=-1, keepdims=True)
        vc = vg - mu
        var = jnp.mean(vc * vc, axis=-1, keepdims=True)
        vn = vc * lax.rsqrt(var + EPS) * lng_ref[...] + lnb_ref[...]
        vnt = [vn[j * LT:(j + 1) * LT, :].T for j in range(GB)]
        st = []
        for g in range(HEADS):
            lhs = jnp.concatenate([vnt[j][g * HEAD_DIM:(g + 1) * HEAD_DIM, :] for j in range(GB)], axis=0)
            st.append(jnp.dot(lhs.astype(BF16), sguw_ref[g], preferred_element_type=F32))
        for j in range(GB):
            stj = jnp.concatenate([st[g][j * HEAD_DIM:(j + 1) * HEAD_DIM, :] for g in range(HEADS)], axis=0)
            sj = stj.T + sgub_ref[...]
            y_scr[j * LT:(j + 1) * LT, D_LRU:] = (ug[j * LT:(j + 1) * LT, :] * sj).astype(BF16)

        mix = jnp.dot(y_scr[...], wout_ref[...], preferred_element_type=F32)
        for j in range(GB):
            b = grp * GB + j
            x1 = x_ref[b] + mod_ref[2, b:b + 1, :] * mix[j * LT:(j + 1) * LT, :]
            x1_ref[b] = x1
            ms = jnp.mean(x1 * x1, axis=-1, keepdims=True)
            h2 = (x1 * lax.rsqrt(ms + EPS)) * mod_ref[4, b:b + 1, :] + mod_ref[3, b:b + 1, :]
            h2_ref[b] = h2
            h_scr[b * LT:(b + 1) * LT, :] = h2.astype(BF16)
        lg = jnp.dot(h_scr[grp * GROWS:(grp + 1) * GROWS, :], wr_ref[...],
                     preferred_element_type=F32) + br_ref[...]
        for j in range(GB):
            b = grp * GB + j
            lg_ref[b] = lg[j * LT:(j + 1) * LT, :]


def _const_spec(shape):
    nd = len(shape)
    return pl.BlockSpec(shape, lambda t, _nd=nd: (0,) * _nd, pipeline_mode=pl.Buffered(1))


def _mixer_call(x, mod, win, convw, convb, wg, bg, lam, lng, lnb, sguw, sgub, wout, wr, br):
    tile = lambda w: pl.BlockSpec((B, LT, w), lambda t: (0, t, 0))
    return pl.pallas_call(
        _mixer_kernel,
        grid=(S // LT,),
        in_specs=[
            tile(D),
            _const_spec((6, B, D)),
            _const_spec((D, 2 * D)),
            _const_spec((CONV_WIDTH, D_LRU)),
            _const_spec((1, D_LRU)),
            _const_spec((D_LRU, 2 * D_LRU)),
            _const_spec((1, 2 * D_LRU)),
            _const_spec((1, D_LRU)),
            _const_spec((1, D_SGU)),
            _const_spec((1, D_SGU)),
            _const_spec((HEADS, SGU_BLOCK, SGU_BLOCK)),
            _const_spec((SGU_BLOCK, D_SGU)),
            _const_spec((D, D)),
            _const_spec((D, LOGIT_W)),
            _const_spec((1, LOGIT_W)),
        ],
        out_specs=[tile(D), tile(D), tile(LOGIT_W)],
        out_shape=[
            jax.ShapeDtypeStruct((B, S, D), F32),
            jax.ShapeDtypeStruct((B, S, D), F32),
            jax.ShapeDtypeStruct((B, S, LOGIT_W), F32),
        ],
        scratch_shapes=[
            pltpu.VMEM((B * LT, D), BF16),
            pltpu.VMEM((B, LT + HALO, D_LRU), F32),
            pltpu.VMEM((NSLAB, B * PITCH, LANES), F32),
            pltpu.VMEM((NSLAB, B * PITCH, LANES), F32),
            pltpu.VMEM((NSLAB, B, LANES), F32),
            pltpu.VMEM((GROWS, D), BF16),
        ],
        compiler_params=pltpu.CompilerParams(
            dimension_semantics=("arbitrary",), vmem_limit_bytes=VMEM_LIMIT),
        name="mixer",
    )(x, mod, win, convw, convb, wg, bg, lam, lng, lnb, sguw, sgub, wout, wr, br)


def _route_kernel(lg_ref, dest_ref, wt_ref, be_ref, cnt_scr, pst_scr, run_scr):
    p = pl.program_id(0)
    i = pl.program_id(1)

    lt = lg_ref[...].T
    row8 = lax.broadcasted_iota(jnp.int32, (SUBLANES, TR), 0)
    gl = jnp.where(row8 < N_GROUPS, lt[0:SUBLANES], -jnp.inf)
    gmax = jnp.max(gl, axis=0, keepdims=True)
    gsel = jnp.min(jnp.where(gl == gmax, row8, SUBLANES), axis=0, keepdims=True)
    pg = 1.0 / jnp.sum(jnp.exp(gl - gmax), axis=0, keepdims=True)
    el = lt[SUBLANES:2 * SUBLANES]
    for g in range(1, N_GROUPS):
        el = jnp.where(gsel == g, lt[(g + 1) * SUBLANES:(g + 2) * SUBLANES], el)
    emax = jnp.max(el, axis=0, keepdims=True)
    pe = jnp.exp(el - emax)
    prob = pe / jnp.sum(pe, axis=0, keepdims=True)
    p0 = jnp.max(prob, axis=0, keepdims=True)
    i0 = jnp.min(jnp.where(prob == p0, row8, SUBLANES), axis=0, keepdims=True)
    pm = jnp.where(row8 == i0, -1.0, prob)
    p1 = jnp.max(pm, axis=0, keepdims=True)
    i1 = jnp.min(jnp.where(pm == p1, row8, SUBLANES), axis=0, keepdims=True)
    den = p0 + p1
    w0 = pg * (p0 / den)
    w1 = pg * (p1 / den)
    e0 = gsel * EPG + i0
    e1 = gsel * EPG + i1

    row32 = lax.broadcasted_iota(jnp.int32, (N_EXPERTS, TR), 0)
    is0 = row32 == e0
    is1 = row32 == e1
    oh = jnp.where(is0, 1.0, 0.0) + jnp.where(is1, 1.0, 0.0)
    ohsum = jnp.sum(oh, axis=1, keepdims=True)

    @pl.when((p == 0) & (i == 0))
    def _zero():
        cnt_scr[...] = jnp.zeros_like(cnt_scr)

    @pl.when(p == 0)
    def _count():
        cnt_scr[...] = cnt_scr[...] + ohsum

    @pl.when((p == 1) & (i == 0))
    def _starts():
        cnt = cnt_scr[...]
        ntile = jnp.ceil(cnt * (1.0 / TM))
        r = lax.broadcasted_iota(jnp.int32, (N_EXPERTS, N_EXPERTS), 0)
        c = lax.broadcasted_iota(jnp.int32, (N_EXPERTS, N_EXPERTS), 1)
        lower = jnp.where(c < r, 1.0, 0.0).astype(BF16)
        tstart = jnp.dot(lower, ntile.astype(BF16), preferred_element_type=F32)
        pst_scr[...] = tstart * TM
        run_scr[...] = jnp.zeros_like(run_scr)
        tend = tstart + ntile
        lane = lax.broadcasted_iota(jnp.int32, (N_EXPERTS, NBLK_PAD), 1).astype(F32)
        be = jnp.sum(jnp.where(tend[:, 0:1] <= lane, 1.0, 0.0), axis=0, keepdims=True)
        be = jnp.minimum(be, N_EXPERTS - 1.0)
        nused = jnp.sum(ntile[:, 0:1], axis=0, keepdims=True)
        rowb = lax.broadcasted_iota(jnp.int32, (SUBLANES, NBLK_PAD), 0)
        be_ref[...] = jnp.where(rowb == 0, be, nused).astype(jnp.int32)

    @pl.when(p == 1)
    def _dest():
        rr = lax.broadcasted_iota(jnp.int32, (TR, TR), 0)
        cc = lax.broadcasted_iota(jnp.int32, (TR, TR), 1)
        upper = jnp.where(rr < cc, 1.0, 0.0).astype(BF16)
        cum = jnp.dot(oh.astype(BF16), upper, preferred_element_type=F32)
        pos = cum + (pst_scr[:, 0:1] + run_scr[:, 0:1])
        d0 = jnp.sum(jnp.where(is0, pos, 0.0), axis=0, keepdims=True)
        d1 = jnp.sum(jnp.where(is1, pos, 0.0), axis=0, keepdims=True)
        run_scr[...] = run_scr[...] + ohsum
        dest_ref[...] = jnp.where(row8 == 0, d0, jnp.where(row8 == 1, d1, 0.0)).astype(jnp.int32)
        wt_ref[...] = jnp.where(row8 == 0, w0, jnp.where(row8 == 1, w1, 0.0))


def _route_call(logits):
    return pl.pallas_call(
        _route_kernel,
        grid=(2, T // TR),
        in_specs=[pl.BlockSpec((TR, LOGIT_W), lambda p, i: (i, 0))],
        out_specs=[
            pl.BlockSpec((SUBLANES, TR), lambda p, i: (0, i * p)),
            pl.BlockSpec((SUBLANES, TR), lambda p, i: (0, i * p)),
            pl.BlockSpec((SUBLANES, NBLK_PAD), lambda p, i: (0, 0)),
        ],
        out_shape=[
            jax.ShapeDtypeStruct((SUBLANES, T), jnp.int32),
            jax.ShapeDtypeStruct((SUBLANES, T), F32),
            jax.ShapeDtypeStruct((SUBLANES, NBLK_PAD), jnp.int32),
        ],
        scratch_shapes=[
            pltpu.VMEM((N_EXPERTS, LANES), F32),
            pltpu.VMEM((N_EXPERTS, LANES), F32),
            pltpu.VMEM((N_EXPERTS, LANES), F32),
        ],
        compiler_params=pltpu.CompilerParams(
            dimension_semantics=("arbitrary", "arbitrary"), vmem_limit_bytes=VMEM_LIMIT),
        name="route",
    )(logits)


def _row_copy(src, src_row, dst, dst_row, sem):
    return pltpu.make_async_copy(src.at[pl.ds(src_row, 1)], dst.at[pl.ds(dst_row, 1)], sem)


def _dispatch_kernel(dest_ref, h2_ref, xs_in_ref, xs_ref, sem):
    del xs_in_ref
    def start(r, carry):
        _row_copy(h2_ref, r, xs_ref, dest_ref[0, r], sem).start()
        _row_copy(h2_ref, r, xs_ref, dest_ref[0, TKD + r], sem).start()
        return carry

    lax.fori_loop(0, TKD, start, 0, unroll=8)

    def wait(r, carry):
        _row_copy(h2_ref, 0, xs_ref, 0, sem).wait()
        _row_copy(h2_ref, 0, xs_ref, 0, sem).wait()
        return carry

    lax.fori_loop(0, TKD, wait, 0)


def _dispatch_call(dest_tiles, h2, xs_zero):
    return pl.pallas_call(
        _dispatch_kernel,
        grid=(T // TKD,),
        in_specs=[
            pl.BlockSpec((None, 1, 2 * TKD), lambda i: (i, 0, 0), memory_space=pltpu.SMEM),
            pl.BlockSpec((TKD, D), lambda i: (i, 0)),
            pl.BlockSpec(memory_space=pl.ANY),
        ],
        out_specs=pl.BlockSpec(memory_space=pl.ANY),
        out_shape=jax.ShapeDtypeStruct((CAP, D), F32),
        scratch_shapes=[pltpu.SemaphoreType.DMA(())],
        input_output_aliases={2: 0},
        compiler_params=pltpu.CompilerParams(
            dimension_semantics=("arbitrary",), vmem_limit_bytes=VMEM_LIMIT),
        name="dispatch",
    )(dest_tiles, h2, xs_zero)


def _experts_kernel(be_ref, nu_ref, xs_ref, w1_ref, w3_ref, w2_ref, ys_ref):
    j = pl.program_id(0)

    @pl.when(j < nu_ref[0])
    def _active():
        xb = xs_ref[...].astype(BF16)
        h1 = jnp.dot(xb, w1_ref[0], preferred_element_type=F32)
        h3 = jnp.dot(xb, w3_ref[0], preferred_element_type=F32)
        act = (h1 * _sigmoid(h1)) * h3
        ys_ref[...] = jnp.dot(act.astype(BF16), w2_ref[0], preferred_element_type=F32)

    @pl.when(j >= nu_ref[0])
    def _idle():
        ys_ref[...] = jnp.zeros_like(ys_ref)


def _experts_call(block_expert, n_used, xs, w1, w3, w2):
    def x_map(j, be, nu):
        return (jnp.minimum(j, nu[0] - 1), 0)

    def w_map(j, be, nu):
        return (be[j], 0, 0)

    grid_spec = pltpu.PrefetchScalarGridSpec(
        num_scalar_prefetch=2,
        grid=(NBLK,),
        in_specs=[
            pl.BlockSpec((TM, D), x_map),
            pl.BlockSpec((1, D, D_EXPERT), w_map),
            pl.BlockSpec((1, D, D_EXPERT), w_map),
            pl.BlockSpec((1, D_EXPERT, D), w_map),
        ],
        out_specs=pl.BlockSpec((TM, D), lambda j, be, nu: (j, 0)),
    )
    return pl.pallas_call(
        _experts_kernel,
        grid_spec=grid_spec,
        out_shape=jax.ShapeDtypeStruct((CAP, D), F32),
        compiler_params=pltpu.CompilerParams(
            dimension_semantics=("arbitrary",), vmem_limit_bytes=VMEM_LIMIT),
        name="experts",
    )(block_expert, n_used, xs, w1, w3, w2)


def _combine_kernel(dcur_ref, dnext_ref, x1_ref, g2_ref, w0_ref, w1_ref, fg_ref, ys_ref,
                    o_ref, ybuf, sems):
    i = pl.program_id(0)
    n = pl.num_programs(0)
    slot = lax.rem(i, 2)

    def gather(dref, sl):
        def body(r, carry):
            _row_copy(ys_ref, dref[0, r], ybuf.at[sl, 0], r, sems.at[sl]).start()
            _row_copy(ys_ref, dref[0, TKC + r], ybuf.at[sl, 1], r, sems.at[sl]).start()
            return carry
        lax.fori_loop(0, TKC, body, 0, unroll=8)

    @pl.when(i == 0)
    def _():
        gather(dcur_ref, 0)

    @pl.when(i + 1 < n)
    def _():
        gather(dnext_ref, 1 - slot)

    def wait(r, carry):
        _row_copy(ys_ref, 0, ybuf.at[slot, 0], 0, sems.at[slot]).wait()
        _row_copy(ys_ref, 0, ybuf.at[slot, 1], 0, sems.at[slot]).wait()
        return carry

    lax.fori_loop(0, TKC, wait, 0)

    y = w0_ref[...] * ybuf[slot, 0] + w1_ref[...] * ybuf[slot, 1]
    x2 = x1_ref[...] + g2_ref[0] * y
    ms = jnp.mean(x2 * x2, axis=-1, keepdims=True)
    o_ref[...] = (x2 * lax.rsqrt(ms + EPS)) * fg_ref[...]


def _combine_call(dest_tiles, x1, mod, w0, w1, final_g, ys):
    nt = T // TKC
    per_b = S // TKC
    return pl.pallas_call(
        _combine_kernel,
        grid=(nt,),
        in_specs=[
            pl.BlockSpec((None, 1, 2 * TKC), lambda i: (i, 0, 0), memory_space=pltpu.SMEM),
            pl.BlockSpec((None, 1, 2 * TKC), lambda i: (jnp.minimum(i + 1, nt - 1), 0, 0),
                         memory_space=pltpu.SMEM),
            pl.BlockSpec((TKC, D), lambda i: (i, 0)),
            pl.BlockSpec((None, 1, 1, D), lambda i: (5, i // per_b, 0, 0)),
            pl.BlockSpec((TKC, 1), lambda i: (i, 0)),
            pl.BlockSpec((TKC, 1), lambda i: (i, 0)),
            pl.BlockSpec((1, D), lambda i: (0, 0)),
            pl.BlockSpec(memory_space=pl.ANY),
        ],
        out_specs=pl.BlockSpec((TKC, D), lambda i: (i, 0)),
        out_shape=jax.ShapeDtypeStruct((T, D), F32),
        scratch_shapes=[
            pltpu.VMEM((2, 2, TKC, D), F32),
            pltpu.SemaphoreType.DMA((2,)),
        ],
        compiler_params=pltpu.CompilerParams(
            dimension_semantics=("arbitrary",), vmem_limit_bytes=VMEM_LIMIT),
        name="combine",
    )(dest_tiles, dest_tiles, x1, mod.reshape(6, B, 1, D), w0, w1, final_g, ys)


def _block_diag(w):
    eye = jnp.eye(HEADS, dtype=w.dtype)
    return jnp.einsum("hde,hg->hdge", w, eye).reshape(D_LRU, D_LRU)


def _dest_tiles(dest, tk):
    nt = T // tk
    return dest[0:2].reshape(2, nt, tk).transpose(1, 0, 2).reshape(nt, 1, 2 * tk)


def kernel(x, c, ada_w, ada_b, norm1_g, w_in, conv_w, conv_b, gate_a_w, gate_a_b, gate_i_w, gate_i_b, lru_lambda, sgu_ln_g, sgu_ln_b, sgu_w, sgu_b, w_out, norm2_g, router_group_w, router_group_b, router_expert_w, router_expert_b, expert_w1, expert_w3, expert_w2, final_g):
    l = 0
    ones = jnp.ones((D,), F32)
    zeros = jnp.zeros((D,), F32)
    mul = jnp.stack([ones, norm1_g[l], ones, ones, norm2_g[l], ones]).reshape(6, 1, D)
    add = jnp.stack([zeros, norm1_g[l], zeros, zeros, norm2_g[l], zeros]).reshape(6, 1, D)
    mod = _mod_call(c, ada_w[l], ada_b[l], mul, add)

    pos_chunk = jnp.arange(SGU_BLOCK) // CHUNK
    mask = (pos_chunk[None, :] <= pos_chunk[:, None]).astype(F32)
    sguw_t = jnp.transpose(sgu_w[l] * mask[None], (0, 2, 1)).astype(BF16)
    sgub_tile = jnp.repeat(sgu_b[l].T, HEAD_DIM, axis=1)
    wg = jnp.concatenate([_block_diag(gate_a_w[l]), _block_diag(gate_i_w[l])], axis=1).astype(BF16)
    bg = jnp.concatenate([gate_a_b[l], gate_i_b[l]]).reshape(1, 2 * D_LRU)
    wr = jnp.zeros((D, LOGIT_W), F32)
    wr = wr.at[:, 0:N_GROUPS].set(router_group_w[l]).at[:, SUBLANES:SUBLANES + N_EXPERTS].set(router_expert_w[l])
    br = jnp.zeros((1, LOGIT_W), F32)
    br = br.at[0, 0:N_GROUPS].set(router_group_b[l]).at[0, SUBLANES:SUBLANES + N_EXPERTS].set(router_expert_b[l])

    x1, h2, logits = _mixer_call(
        x, mod, w_in[l].astype(BF16), conv_w[l], conv_b[l].reshape(1, D_LRU), wg, bg,
        lru_lambda[l].reshape(1, D_LRU), sgu_ln_g[l].reshape(1, D_SGU), sgu_ln_b[l].reshape(1, D_SGU),
        sguw_t, sgub_tile, w_out[l].astype(BF16), wr.astype(BF16), br)

    dest, wts, be = _route_call(logits.reshape(T, LOGIT_W))
    block_expert = be[0, 0:NBLK]
    n_used = be[1, 0:1]

    h2 = h2.reshape(T, D)
    xs = _dispatch_call(_dest_tiles(dest, TKD), h2, jnp.zeros((CAP, D), F32))
    ys = _experts_call(block_expert, n_used, xs, expert_w1[l].astype(BF16), expert_w3[l].astype(BF16),
                       expert_w2[l].astype(BF16))
    out = _combine_call(_dest_tiles(dest, TKC), x1.reshape(T, D), mod, wts[0].reshape(T, 1),
                        wts[1].reshape(T, 1), final_g.reshape(1, D), ys)
    return out.reshape(B, S, D)
```

```python
import functools

import jax
import jax.numpy as jnp
from jax import lax
from jax.experimental import pallas as pl
from jax.experimental.pallas import tpu as pltpu
from jax.experimental.pallas import tpu_sc as plsc

F32 = jnp.float32
BF16 = jnp.bfloat16
I32 = jnp.int32
U32 = jnp.uint32

D = 1024
B = 8
S = 4096
T = B * S
D_LRU = 512
D_SGU = 512
HEADS = 8
HEAD_DIM = 64
CONV_WIDTH = 4
LRU_C = 8.0
SGU_BLOCK = 128
CHUNK = 64
N_GROUPS = 4
EPG = 8
N_EXPERTS = N_GROUPS * EPG
TOP_K = 2
D_EXPERT = 512
EPS = 1e-6
DP = D // 2

LANES = 128
SUBLANES = 8
VMEM_LIMIT = 60 * 1024 * 1024

SC_CORES = 2
SC_SUBCORES = 16
SC_WORKERS = SC_CORES * SC_SUBCORES
SC_ROWS = 64
SC_TOK = T // SC_WORKERS
SC_STEPS = SC_TOK // SC_ROWS

LT = SGU_BLOCK
GB = 4
NGRP = B // GB
GROWS = GB * LT
PITCH = LT + SUBLANES
HALO = SUBLANES
NSLAB = D_LRU // LANES

TM = 512
N_SLOTS = T * TOP_K
CAP = N_SLOTS + N_EXPERTS * TM
NBLK = CAP // TM
NBLK_PAD = 256
TR = 1024
TKC = 512
LOGIT_W = LANES


def _gelu(x):
    return 0.5 * x * (1.0 + jnp.tanh(0.7978845608028654 * (x + 0.044715 * (x * x * x))))


def _sigmoid(x):
    return 1.0 / (1.0 + jnp.exp(-x))


def _pack_rows(x):
    lo = lax.bitcast_convert_type(x[:, 0:DP].astype(BF16).astype(F32), U32)
    hi = lax.bitcast_convert_type(x[:, DP:D].astype(BF16).astype(F32), U32)
    word = hi | (lo >> 16)
    return lax.bitcast_convert_type(word, I32)


def _unpack_rows(w):
    u = lax.bitcast_convert_type(w, U32)
    lo = lax.bitcast_convert_type(u << 16, F32)
    hi = lax.bitcast_convert_type(u & jnp.uint32(0xFFFF0000), F32)
    return jnp.concatenate([lo, hi], axis=1)


def _mod_kernel(c_ref, w_ref, b_ref, mul_ref, add_ref, o_ref):
    c = c_ref[...]
    cond = c * _sigmoid(c)
    m = jnp.dot(cond, w_ref[...], preferred_element_type=F32, precision=lax.Precision.HIGHEST)
    m = m + b_ref[0]
    o_ref[0] = m * mul_ref[0] + add_ref[0]


def _mod_call(c, ada_w, ada_b, mul, add):
    return pl.pallas_call(
        _mod_kernel,
        grid=(6,),
        in_specs=[
            pl.BlockSpec((B, D), lambda k: (0, 0)),
            pl.BlockSpec((D, D), lambda k: (0, k)),
            pl.BlockSpec((1, 1, D), lambda k: (k, 0, 0)),
            pl.BlockSpec((1, 1, D), lambda k: (k, 0, 0)),
            pl.BlockSpec((1, 1, D), lambda k: (k, 0, 0)),
        ],
        out_specs=pl.BlockSpec((1, B, D), lambda k: (k, 0, 0)),
        out_shape=jax.ShapeDtypeStruct((6, B, D), F32),
        name="mod",
    )(c, ada_w, ada_b.reshape(6, 1, D), mul, add)


def _mixer_kernel(x_ref, mod_ref, win_ref, convw_ref, convb_ref, wg_ref, bg_ref, lam_ref,
                  lng_ref, lnb_ref, sguw_ref, sgub_ref, wout_ref, wr_ref, br_ref,
                  x1_ref, h2_ref, lg_ref,
                  h_scr, xr_scr, a_scr, u_scr, hc_scr, y_scr):
    t = pl.program_id(0)

    @pl.when(t == 0)
    def _init():
        hc_scr[...] = jnp.zeros_like(hc_scr)
        xr_scr[:, 0:HALO, :] = jnp.zeros((B, HALO, D_LRU), F32)

    lam = lam_ref[...]
    z = -lam
    softplus = jnp.maximum(z, 0.0) + jnp.log1p(jnp.exp(-jnp.abs(z)))
    clam = -LRU_C * softplus
    cw = convw_ref[...]
    cb = convb_ref[...]

    for grp in range(NGRP):
        for j in range(GB):
            b = grp * GB + j
            xb = x_ref[b]
            ms = jnp.mean(xb * xb, axis=-1, keepdims=True)
            hb = (xb * lax.rsqrt(ms + EPS)) * mod_ref[1, b:b + 1, :] + mod_ref[0, b:b + 1, :]
            h_scr[b * LT:(b + 1) * LT, :] = hb.astype(BF16)
        hg = h_scr[grp * GROWS:(grp + 1) * GROWS, :]
        xr = jnp.dot(hg, win_ref[:, 0:D_LRU], preferred_element_type=F32)
        xcs = []
        for j in range(GB):
            b = grp * GB + j
            xr_scr[b, HALO:HALO + LT, :] = xr[j * LT:(j + 1) * LT, :]
            acc = cb + cw[CONV_WIDTH - 1:CONV_WIDTH] * xr_scr[b, HALO:HALO + LT, :]
            for k in range(CONV_WIDTH - 1):
                off = HALO - (CONV_WIDTH - 1) + k
                acc = acc + cw[k:k + 1] * xr_scr[b, off:off + LT, :]
            xcs.append(acc)
            xr_scr[b, 0:HALO, :] = xr_scr[b, LT:LT + HALO, :]
        xc = jnp.concatenate(xcs, axis=0)
        gpre = jnp.dot(xc.astype(BF16), wg_ref[...], preferred_element_type=F32) + bg_ref[...]
        r = _sigmoid(gpre[:, 0:D_LRU])
        gi = _sigmoid(gpre[:, D_LRU:2 * D_LRU])
        a = jnp.exp(clam * r)
        uin = jnp.sqrt(1.0 - a * a) * (gi * xc)
        for j in range(GB):
            b = grp * GB + j
            for s in range(NSLAB):
                a_scr[s, b * PITCH:b * PITCH + LT, :] = a[j * LT:(j + 1) * LT, s * LANES:(s + 1) * LANES]
                u_scr[s, b * PITCH:b * PITCH + LT, :] = uin[j * LT:(j + 1) * LT, s * LANES:(s + 1) * LANES]

    def scan_body(tt, carry):
        out = []
        for s in range(NSLAB):
            at = a_scr[s, pl.ds(tt, B, stride=PITCH), :]
            ut = u_scr[s, pl.ds(tt, B, stride=PITCH), :]
            hn = at * carry[s] + ut
            u_scr[s, pl.ds(tt, B, stride=PITCH), :] = hn
            out.append(hn)
        return tuple(out)

    carry = lax.fori_loop(0, LT, scan_body, tuple(hc_scr[s] for s in range(NSLAB)), unroll=8)
    for s in range(NSLAB):
        hc_scr[s] = carry[s]

    for grp in range(NGRP):
        hg = h_scr[grp * GROWS:(grp + 1) * GROWS, :]
        gr = jnp.dot(hg, win_ref[:, D_LRU:2 * D_LRU], preferred_element_type=F32)
        gg = _gelu(gr)
        for j in range(GB):
            b = grp * GB + j
            hl = jnp.concatenate([u_scr[s, b * PITCH:b * PITCH + LT, :] for s in range(NSLAB)], axis=1)
            y_scr[j * LT:(j + 1) * LT, 0:D_LRU] = (hl * gg[j * LT:(j + 1) * LT, :]).astype(BF16)

        su = jnp.dot(hg, win_ref[:, 2 * D_LRU:2 * D_LRU + D_SGU], preferred_element_type=F32)
        sv = jnp.dot(hg, win_ref[:, 2 * D_LRU + D_SGU:], preferred_element_type=F32)
        ug = _gelu(su)
        vg = _gelu(sv)
        mu = jnp.mean(vg, axis=-1, keepdims=True)
        vc = vg - mu
        var = jnp.mean(vc * vc, axis=-1, keepdims=True)
        vn = vc * lax.rsqrt(var + EPS) * lng_ref[...] + lnb_ref[...]
        vnt = [vn[j * LT:(j + 1) * LT, :].T for j in range(GB)]
        st = []
        for g in range(HEADS):
            lhs = jnp.concatenate([vnt[j][g * HEAD_DIM:(g + 1) * HEAD_DIM, :] for j in range(GB)], axis=0)
            st.append(jnp.dot(lhs.astype(BF16), sguw_ref[g], preferred_element_type=F32))
        for j in range(GB):
            stj = jnp.concatenate([st[g][j * HEAD_DIM:(j + 1) * HEAD_DIM, :] for g in range(HEADS)], axis=0)
            sj = stj.T + sgub_ref[...]
            y_scr[j * LT:(j + 1) * LT, D_LRU:] = (ug[j * LT:(j + 1) * LT, :] * sj).astype(BF16)

        mix = jnp.dot(y_scr[...], wout_ref[...], preferred_element_type=F32)
        for j in range(GB):
            b = grp * GB + j
            x1 = x_ref[b] + mod_ref[2, b:b + 1, :] * mix[j * LT:(j + 1) * LT, :]
            x1_ref[b] = x1
            ms = jnp.mean(x1 * x1, axis=-1, keepdims=True)
            h2 = (x1 * lax.rsqrt(ms + EPS)) * mod_ref[4, b:b + 1, :] + mod_ref[3, b:b + 1, :]
            h2_ref[b] = _pack_rows(h2)
            h_scr[b * LT:(b + 1) * LT, :] = h2.astype(BF16)
        lg = jnp.dot(h_scr[grp * GROWS:(grp + 1) * GROWS, :], wr_ref[...],
                     preferred_element_type=F32) + br_ref[...]
        for j in range(GB):
            b = grp * GB + j
            lg_ref[b] = lg[j * LT:(j + 1) * LT, :]


def _const_spec(shape):
    nd = len(shape)
    return pl.BlockSpec(shape, lambda t, _nd=nd: (0,) * _nd, pipeline_mode=pl.Buffered(1))


def _mixer_call(x, mod, win, convw, convb, wg, bg, lam, lng, lnb, sguw, sgub, wout, wr, br):
    tile = lambda w: pl.BlockSpec((B, LT, w), lambda t: (0, t, 0))
    return pl.pallas_call(
        _mixer_kernel,
        grid=(S // LT,),
        in_specs=[
            tile(D),
            _const_spec((6, B, D)),
            _const_spec((D, 2 * D)),
            _const_spec((CONV_WIDTH, D_LRU)),
            _const_spec((1, D_LRU)),
            _const_spec((D_LRU, 2 * D_LRU)),
            _const_spec((1, 2 * D_LRU)),
            _const_spec((1, D_LRU)),
            _const_spec((1, D_SGU)),
            _const_spec((1, D_SGU)),
            _const_spec((HEADS, SGU_BLOCK, SGU_BLOCK)),
            _const_spec((SGU_BLOCK, D_SGU)),
            _const_spec((D, D)),
            _const_spec((D, LOGIT_W)),
            _const_spec((1, LOGIT_W)),
        ],
        out_specs=[tile(D), tile(DP), tile(LOGIT_W)],
        out_shape=[
            jax.ShapeDtypeStruct((B, S, D), F32),
            jax.ShapeDtypeStruct((B, S, DP), I32),
            jax.ShapeDtypeStruct((B, S, LOGIT_W), F32),
        ],
        scratch_shapes=[
            pltpu.VMEM((B * LT, D), BF16),
            pltpu.VMEM((B, LT + HALO, D_LRU), F32),
            pltpu.VMEM((NSLAB, B * PITCH, LANES), F32),
            pltpu.VMEM((NSLAB, B * PITCH, LANES), F32),
            pltpu.VMEM((NSLAB, B, LANES), F32),
            pltpu.VMEM((GROWS, D), BF16),
        ],
        compiler_params=pltpu.CompilerParams(
            dimension_semantics=("arbitrary",), vmem_limit_bytes=VMEM_LIMIT),
        name="mixer",
    )(x, mod, win, convw, convb, wg, bg, lam, lng, lnb, sguw, sgub, wout, wr, br)


def _route_kernel(lg_ref, dest_ref, wt_ref, be_ref, cnt_scr, pst_scr, run_scr):
    p = pl.program_id(0)
    i = pl.program_id(1)

    lt = lg_ref[...].T
    row8 = lax.broadcasted_iota(I32, (SUBLANES, TR), 0)
    gl = jnp.where(row8 < N_GROUPS, lt[0:SUBLANES], -jnp.inf)
    gmax = jnp.max(gl, axis=0, keepdims=True)
    gsel = jnp.min(jnp.where(gl == gmax, row8, SUBLANES), axis=0, keepdims=True)
    pg = 1.0 / jnp.sum(jnp.exp(gl - gmax), axis=0, keepdims=True)
    el = lt[SUBLANES:2 * SUBLANES]
    for g in range(1, N_GROUPS):
        el = jnp.where(gsel == g, lt[(g + 1) * SUBLANES:(g + 2) * SUBLANES], el)
    emax = jnp.max(el, axis=0, keepdims=True)
    pe = jnp.exp(el - emax)
    prob = pe / jnp.sum(pe, axis=0, keepdims=True)
    p0 = jnp.max(prob, axis=0, keepdims=True)
    i0 = jnp.min(jnp.where(prob == p0, row8, SUBLANES), axis=0, keepdims=True)
    pm = jnp.where(row8 == i0, -1.0, prob)
    p1 = jnp.max(pm, axis=0, keepdims=True)
    i1 = jnp.min(jnp.where(pm == p1, row8, SUBLANES), axis=0, keepdims=True)
    den = p0 + p1
    w0 = pg * (p0 / den)
    w1 = pg * (p1 / den)
    e0 = gsel * EPG + i0
    e1 = gsel * EPG + i1

    row32 = lax.broadcasted_iota(I32, (N_EXPERTS, TR), 0)
    is0 = row32 == e0
    is1 = row32 == e1
    oh = jnp.where(is0, 1.0, 0.0) + jnp.where(is1, 1.0, 0.0)
    ohsum = jnp.sum(oh, axis=1, keepdims=True)

    @pl.when((p == 0) & (i == 0))
    def _zero():
        cnt_scr[...] = jnp.zeros_like(cnt_scr)

    @pl.when(p == 0)
    def _count():
        cnt_scr[...] = cnt_scr[...] + ohsum

    @pl.when((p == 1) & (i == 0))
    def _starts():
        cnt = cnt_scr[...]
        ntile = jnp.ceil(cnt * (1.0 / TM))
        r = lax.broadcasted_iota(I32, (N_EXPERTS, N_EXPERTS), 0)
        c = lax.broadcasted_iota(I32, (N_EXPERTS, N_EXPERTS), 1)
        lower = jnp.where(c < r, 1.0, 0.0).astype(BF16)
        tstart = jnp.dot(lower, ntile.astype(BF16), preferred_element_type=F32)
        pst_scr[...] = tstart * TM
        run_scr[...] = jnp.zeros_like(run_scr)
        tend = tstart + ntile
        lane = lax.broadcasted_iota(I32, (N_EXPERTS, NBLK_PAD), 1).astype(F32)
        ts1 = tstart[:, 0:1]
        te1 = tend[:, 0:1]
        be = jnp.sum(jnp.where(te1 <= lane, 1.0, 0.0), axis=0, keepdims=True)
        be = jnp.minimum(be, N_EXPERTS - 1.0)
        nused = jnp.sum(ntile[:, 0:1], axis=0, keepdims=True)
        left = jnp.clip(cnt[:, 0:1] - (lane - ts1) * TM, 0.0, TM * 1.0)
        nvalid = jnp.sum(jnp.where((ts1 <= lane) & (lane < te1), left, 0.0), axis=0, keepdims=True)
        rowb = lax.broadcasted_iota(I32, (SUBLANES, NBLK_PAD), 0)
        be_ref[...] = jnp.where(rowb == 0, be, jnp.where(rowb == 1, nused, nvalid)).astype(I32)

    @pl.when(p == 1)
    def _dest():
        rr = lax.broadcasted_iota(I32, (TR, TR), 0)
        cc = lax.broadcasted_iota(I32, (TR, TR), 1)
        upper = jnp.where(rr < cc, 1.0, 0.0).astype(BF16)
        cum = jnp.dot(oh.astype(BF16), upper, preferred_element_type=F32)
        pos = cum + (pst_scr[:, 0:1] + run_scr[:, 0:1])
        d0 = jnp.sum(jnp.where(is0, pos, 0.0), axis=0, keepdims=True)
        d1 = jnp.sum(jnp.where(is1, pos, 0.0), axis=0, keepdims=True)
        run_scr[...] = run_scr[...] + ohsum
        dest_ref[...] = jnp.where(row8 == 0, d0, jnp.where(row8 == 1, d1, 0.0)).astype(I32)
        wt_ref[...] = jnp.where(row8 == 0, w0, jnp.where(row8 == 1, w1, 0.0))


def _route_call(logits):
    return pl.pallas_call(
        _route_kernel,
        grid=(2, T // TR),
        in_specs=[pl.BlockSpec((TR, LOGIT_W), lambda p, i: (i, 0))],
        out_specs=[
            pl.BlockSpec((SUBLANES, TR), lambda p, i: (0, i * p)),
            pl.BlockSpec((SUBLANES, TR), lambda p, i: (0, i * p)),
            pl.BlockSpec((SUBLANES, NBLK_PAD), lambda p, i: (0, 0)),
        ],
        out_shape=[
            jax.ShapeDtypeStruct((SUBLANES, T), I32),
            jax.ShapeDtypeStruct((SUBLANES, T), F32),
            jax.ShapeDtypeStruct((SUBLANES, NBLK_PAD), I32),
        ],
        scratch_shapes=[
            pltpu.VMEM((N_EXPERTS, LANES), F32),
            pltpu.VMEM((N_EXPERTS, LANES), F32),
            pltpu.VMEM((N_EXPERTS, LANES), F32),
        ],
        compiler_params=pltpu.CompilerParams(
            dimension_semantics=("arbitrary", "arbitrary"), vmem_limit_bytes=VMEM_LIMIT),
        name="route",
    )(logits)


def _sc_mesh():
    return plsc.VectorSubcoreMesh(core_axis_name="c", subcore_axis_name="s")


def _sc_worker():
    return lax.axis_index("s") * SC_CORES + lax.axis_index("c")


def _dispatch_kernel(h2_hbm, d0_hbm, d1_hbm, xs_hbm, idx0_v, idx1_v, rows_v, sem):
    wid = _sc_worker()
    pltpu.sync_copy(d0_hbm.at[wid], idx0_v)
    pltpu.sync_copy(d1_hbm.at[wid], idx1_v)
    base = wid * SC_TOK
    for j in range(SC_STEPS):
        start = pl.multiple_of(base + j * SC_ROWS, SC_ROWS)
        pltpu.sync_copy(h2_hbm.at[pl.ds(start, SC_ROWS)], rows_v)
        c0 =pltpu.async_copy(rows_v, xs_hbm.at[idx0_v.at[j]], sem)
        c1 = pltpu.async_copy(rows_v, xs_hbm.at[idx1_v.at[j]], sem)
        c0.wait()
        c1.wait()


def _dispatch_call(h2p, d0, d1):
    return pl.kernel(
        _dispatch_kernel,
        out_type=jax.ShapeDtypeStruct((CAP, DP), I32),
        mesh=_sc_mesh(),
        scratch_types=[
            pltpu.VMEM((SC_STEPS, SC_ROWS), I32),
            pltpu.VMEM((SC_STEPS, SC_ROWS), I32),
            pltpu.VMEM((SC_ROWS, DP), I32),
            pltpu.SemaphoreType.DMA,
        ],
        name="dispatch",
    )(h2p, d0, d1)


def _gather_kernel(ys_hbm, d0_hbm, d1_hbm, y0_hbm, y1_hbm, idx0_v, idx1_v, rows_v, sem):
    wid = _sc_worker()
    pltpu.sync_copy(d0_hbm.at[wid], idx0_v)
    pltpu.sync_copy(d1_hbm.at[wid], idx1_v)
    base = wid * SC_TOK
    for j in range(SC_STEPS):
        start = pl.multiple_of(base + j * SC_ROWS, SC_ROWS)
        pltpu.async_copy(ys_hbm.at[idx0_v.at[j]], rows_v, sem).wait()
        pltpu.sync_copy(rows_v, y0_hbm.at[pl.ds(start, SC_ROWS)])
        pltpu.async_copy(ys_hbm.at[idx1_v.at[j]], rows_v, sem).wait()
        pltpu.sync_copy(rows_v, y1_hbm.at[pl.ds(start, SC_ROWS)])


def _gather_call(ys, d0, d1):
    row = jax.ShapeDtypeStruct((T, DP), I32)
    return pl.kernel(
        _gather_kernel,
        out_type=(row, row),
        mesh=_sc_mesh(),
        scratch_types=[
            pltpu.VMEM((SC_STEPS, SC_ROWS), I32),
            pltpu.VMEM((SC_STEPS, SC_ROWS), I32),
            pltpu.VMEM((SC_ROWS, DP), I32),
            pltpu.SemaphoreType.DMA,
        ],
        name="gather",
    )(ys, d0, d1)


def _experts_kernel(be_ref, nu_ref, nv_ref, xs_ref, w1_ref, w3_ref, w2_ref, ys_ref):
    j = pl.program_id(0)

    @pl.when(j < nu_ref[0])
    def _active():
        row = lax.broadcasted_iota(I32, (TM, DP), 0)
        words = jnp.where(row < nv_ref[j], xs_ref[...], 0)
        xb = _unpack_rows(words).astype(BF16)
        h1 = jnp.dot(xb, w1_ref[0], preferred_element_type=F32)
        h3 = jnp.dot(xb, w3_ref[0], preferred_element_type=F32)
        act = (h1 * _sigmoid(h1)) * h3
        y = jnp.dot(act.astype(BF16), w2_ref[0], preferred_element_type=F32)
        ys_ref[...] = _pack_rows(y)


def _experts_call(block_expert, n_used, n_valid, xs, w1, w3, w2):
    def x_map(j, be, nu, nv):
        return (jnp.minimum(j, nu[0] - 1), 0)

    def w_map(j, be, nu, nv):
        return (be[jnp.minimum(j, nu[0] - 1)], 0, 0)

    grid_spec = pltpu.PrefetchScalarGridSpec(
        num_scalar_prefetch=3,
        grid=(NBLK,),
        in_specs=[
            pl.BlockSpec((TM, DP), x_map),
            pl.BlockSpec((1, D, D_EXPERT), w_map),
            pl.BlockSpec((1, D, D_EXPERT), w_map),
            pl.BlockSpec((1, D_EXPERT, D), w_map),
        ],
        out_specs=pl.BlockSpec((TM, DP), x_map),
    )
    return pl.pallas_call(
        _experts_kernel,
        grid_spec=grid_spec,
        out_shape=jax.ShapeDtypeStruct((CAP, DP), I32),
        compiler_params=pltpu.CompilerParams(
            dimension_semantics=("arbitrary",), vmem_limit_bytes=VMEM_LIMIT),
        name="experts",
    )(block_expert, n_used, n_valid, xs, w1, w3, w2)


def _combine_kernel(x1_ref, g2_ref, w0_ref, w1_ref, fg_ref, y0_ref, y1_ref, o_ref):
    y = w0_ref[...] * _unpack_rows(y0_ref[...]) + w1_ref[...] * _unpack_rows(y1_ref[...])
    x2 = x1_ref[...] + g2_ref[0] * y
    ms = jnp.mean(x2 * x2, axis=-1, keepdims=True)
    o_ref[...] = (x2 * lax.rsqrt(ms + EPS)) * fg_ref[...]


def _combine_call(x1, mod, w0, w1, final_g, y0, y1):
    per_b = S // TKC
    return pl.pallas_call(
        _combine_kernel,
        grid=(T // TKC,),
        in_specs=[
            pl.BlockSpec((TKC, D), lambda i: (i, 0)),
            pl.BlockSpec((None, 1, 1, D), lambda i: (5, i // per_b, 0, 0)),
            pl.BlockSpec((TKC, 1), lambda i: (i, 0)),
            pl.BlockSpec((TKC, 1), lambda i: (i, 0)),
            pl.BlockSpec((1, D), lambda i: (0, 0)),
            pl.BlockSpec((TKC, DP), lambda i: (i, 0)),
            pl.BlockSpec((TKC, DP), lambda i: (i, 0)),
        ],
        out_specs=pl.BlockSpec((TKC, D), lambda i: (i, 0)),
        out_shape=jax.ShapeDtypeStruct((T, D), F32),
        compiler_params=pltpu.CompilerParams(
            dimension_semantics=("arbitrary",), vmem_limit_bytes=VMEM_LIMIT),
        name="combine",
    )(x1, mod.reshape(6, B, 1, D), w0, w1, final_g, y0, y1)


def _block_diag(w):
    eye = jnp.eye(HEADS, dtype=w.dtype)
    return jnp.einsum("hde,hg->hdge", w, eye).reshape(D_LRU, D_LRU)


def kernel(x, c, ada_w, ada_b, norm1_g, w_in, conv_w, conv_b, gate_a_w, gate_a_b, gate_i_w, gate_i_b, lru_lambda, sgu_ln_g, sgu_ln_b, sgu_w, sgu_b, w_out, norm2_g, router_group_w, router_group_b, router_expert_w, router_expert_b, expert_w1, expert_w3, expert_w2, final_g):
    l = 0
    ones = jnp.ones((D,), F32)
    zeros = jnp.zeros((D,), F32)
    mul = jnp.stack([ones, norm1_g[l], ones, ones, norm2_g[l], ones]).reshape(6, 1, D)
    add = jnp.stack([zeros, norm1_g[l], zeros, zeros, norm2_g[l], zeros]).reshape(6, 1, D)
    mod = _mod_call(c, ada_w[l], ada_b[l], mul, add)

    pos_chunk = jnp.arange(SGU_BLOCK) // CHUNK
    mask = (pos_chunk[None, :] <= pos_chunk[:, None]).astype(F32)
    sguw_t = jnp.transpose(sgu_w[l] * mask[None], (0, 2, 1)).astype(BF16)
    sgub_tile = jnp.repeat(sgu_b[l].T, HEAD_DIM, axis=1)
    wg = jnp.concatenate([_block_diag(gate_a_w[l]), _block_diag(gate_i_w[l])], axis=1).astype(BF16)
    bg = jnp.concatenate([gate_a_b[l], gate_i_b[l]]).reshape(1, 2 * D_LRU)
    wr = jnp.zeros((D, LOGIT_W), F32)
    wr = wr.at[:, 0:N_GROUPS].set(router_group_w[l]).at[:, SUBLANES:SUBLANES + N_EXPERTS].set(router_expert_w[l])
    br = jnp.zeros((1, LOGIT_W), F32)
    br = br.at[0, 0:N_GROUPS].set(router_group_b[l]).at[0, SUBLANES:SUBLANES + N_EXPERTS].set(router_expert_b[l])

    x1, h2p, logits = _mixer_call(
        x, mod, w_in[l].astype(BF16), conv_w[l], conv_b[l].reshape(1, D_LRU), wg, bg,
        lru_lambda[l].reshape(1, D_LRU), sgu_ln_g[l].reshape(1, D_SGU), sgu_ln_b[l].reshape(1, D_SGU),
        sguw_t, sgub_tile, w_out[l].astype(BF16), wr.astype(BF16), br)

    dest, wts, be = _route_call(logits.reshape(T, LOGIT_W))
    block_expert = be[0, 0:NBLK]
    n_used = be[1, 0:1]
    n_valid = be[2, 0:NBLK]
    d0 = dest[0].reshape(SC_WORKERS, SC_STEPS, SC_ROWS)
    d1 = dest[1].reshape(SC_WORKERS, SC_STEPS, SC_ROWS)

    xs = _dispatch_call(h2p.reshape(T, DP), d0, d1)
    ys = _experts_call(block_expert, n_used, n_valid, xs, expert_w1[l].astype(BF16),
                       expert_w3[l].astype(BF16), expert_w2[l].astype(BF16))
    y0, y1 = _gather_call(ys, d0, d1)
    out = _combine_call(x1.reshape(T, D), mod, wts[0].reshape(T, 1), wts[1].reshape(T, 1),
                        final_g.reshape(1, D), y0, y1)
    return out.reshape(B, S, D)
```

```python
import functools

import jax
import jax.numpy as jnp
from jax import lax
from jax.experimental import pallas as pl
from jax.experimental.pallas import tpu as pltpu
from jax.experimental.pallas import tpu_sc as plsc

F32 = jnp.float32
BF16 = jnp.bfloat16
I32 = jnp.int32
U32 = jnp.uint32

D = 1024
B = 8
S = 4096
T = B * S
D_LRU = 512
D_SGU = 512
HEADS = 8
HEAD_DIM = 64
CONV_WIDTH = 4
LRU_C = 8.0
SGU_BLOCK = 128
CHUNK = 64
N_GROUPS = 4
EPG = 8
N_EXPERTS = N_GROUPS * EPG
TOP_K = 2
D_EXPERT = 512
EPS = 1e-6
DP = D // 2

LANES = 128
SUBLANES = 8
VMEM_LIMIT = 60 * 1024 * 1024

NH = 2
BH = B // NH
TH = BH * S

SC_CORES = 2
SC_SUBCORES = 16
SC_WORKERS = SC_CORES * SC_SUBCORES
SC_ROWS = 64
SC_TOK = TH // SC_WORKERS
SC_STEPS = SC_TOK // SC_ROWS

LT = SGU_BLOCK
ROWS = BH * LT
PITCH = LT + SUBLANES
HALO = SUBLANES
NSLAB = D_LRU // LANES

TM = 512
N_SLOTS = TH * TOP_K
CAP = N_SLOTS + N_EXPERTS * TM
NBLK = CAP // TM
NBLK_PAD = LANES
TR = 1024
TKC = 512
LOGIT_W = LANES


def _gelu(x):
    return 0.5 * x * (1.0 + jnp.tanh(0.7978845608028654 * (x + 0.044715 * (x * x * x))))


def _sigmoid(x):
    return 1.0 / (1.0 + jnp.exp(-x))


def _pack_rows(x):
    lo = lax.bitcast_convert_type(x[:, 0:DP].astype(BF16).astype(F32), U32)
    hi = lax.bitcast_convert_type(x[:, DP:D].astype(BF16).astype(F32), U32)
    word = hi | (lo >> 16)
    return lax.bitcast_convert_type(word, I32)


def _unpack_rows(w):
    u = lax.bitcast_convert_type(w, U32)
    lo = lax.bitcast_convert_type(u << 16, F32)
    hi = lax.bitcast_convert_type(u & jnp.uint32(0xFFFF0000), F32)
    return jnp.concatenate([lo, hi], axis=1)


def _mod_kernel(c_ref, w_ref, b_ref, mul_ref, add_ref, o_ref):
    c = c_ref[...]
    cond = c * _sigmoid(c)
    m = jnp.dot(cond, w_ref[...], preferred_element_type=F32, precision=lax.Precision.HIGHEST)
    m = m + b_ref[0]
    o_ref[0] = m * mul_ref[0] + add_ref[0]


def _mod_call(c, ada_w, ada_b, mul, add):
    return pl.pallas_call(
        _mod_kernel,
        grid=(6,),
        in_specs=[
            pl.BlockSpec((B, D), lambda k: (0, 0)),
            pl.BlockSpec((D, D), lambda k: (0, k)),
            pl.BlockSpec((1, 1, D), lambda k: (k, 0, 0)),
            pl.BlockSpec((1, 1, D), lambda k: (k, 0, 0)),
            pl.BlockSpec((1, 1, D), lambda k: (k, 0, 0)),
        ],
        out_specs=pl.BlockSpec((1, B, D), lambda k: (k, 0, 0)),
        out_shape=jax.ShapeDtypeStruct((6, B, D), F32),
        name="mod",
    )(c, ada_w, ada_b.reshape(6, 1, D), mul, add)


def _mixer_kernel(x_ref, mod_ref, win_ref, convw_ref, convb_ref, wg_ref, bg_ref, lam_ref,
                  lng_ref, lnb_ref, sguw_ref, sgub_ref, wout_ref, wr_ref, br_ref,
                  x1_ref, h2_ref, lg_ref,
                  h_scr, xr_scr, a_scr, u_scr, hc_scr, y_scr, *, half):
    t = pl.program_id(0)

    @pl.when(t == 0)
    def _init():
        hc_scr[...] = jnp.zeros_like(hc_scr)
        xr_scr[:, 0:HALO, :] = jnp.zeros((BH, HALO, D_LRU), F32)

    def mod_row(k, j):
        b = half * BH + j
        return mod_ref[k, b:b + 1, :]

    lam = lam_ref[...]
    z = -lam
    softplus = jnp.maximum(z, 0.0) + jnp.log1p(jnp.exp(-jnp.abs(z)))
    clam = -LRU_C * softplus
    cw = convw_ref[...]
    cb = convb_ref[...]

    for j in range(BH):
        xb = x_ref[j]
        ms = jnp.mean(xb * xb, axis=-1, keepdims=True)
        hb = (xb * lax.rsqrt(ms + EPS)) * mod_row(1, j) + mod_row(0, j)
        h_scr[j * LT:(j + 1) * LT, :] = hb.astype(BF16)
    hg = h_scr[...]
    xr = jnp.dot(hg, win_ref[:, 0:D_LRU], preferred_element_type=F32)
    xcs = []
    for j in range(BH):
        xr_scr[j, HALO:HALO + LT, :] = xr[j * LT:(j + 1) * LT, :]
        acc = cb + cw[CONV_WIDTH - 1:CONV_WIDTH] * xr_scr[j, HALO:HALO + LT, :]
        for k in range(CONV_WIDTH - 1):
            off = HALO - (CONV_WIDTH - 1) + k
            acc = acc + cw[k:k + 1] * xr_scr[j, off:off + LT, :]
        xcs.append(acc)
        xr_scr[j, 0:HALO, :] = xr_scr[j, LT:LT + HALO, :]
    xc = jnp.concatenate(xcs, axis=0)
    gpre = jnp.dot(xc.astype(BF16), wg_ref[...], preferred_element_type=F32) + bg_ref[...]
    r = _sigmoid(gpre[:, 0:D_LRU])
    gi = _sigmoid(gpre[:, D_LRU:2 * D_LRU])
    a = jnp.exp(clam * r)
    uin = jnp.sqrt(1.0 - a * a) * (gi * xc)
    for j in range(BH):
        for s in range(NSLAB):
            a_scr[s, j * PITCH:j * PITCH + LT, :] = a[j * LT:(j + 1) * LT, s * LANES:(s + 1) * LANES]
            u_scr[s, j * PITCH:j * PITCH + LT, :] = uin[j * LT:(j + 1) * LT, s * LANES:(s + 1) * LANES]

    def scan_body(tt, carry):
        out = []
        for s in range(NSLAB):
            at = a_scr[s, pl.ds(tt, BH, stride=PITCH), :]
            ut = u_scr[s, pl.ds(tt, BH, stride=PITCH), :]
            hn = at * carry[s] + ut
            u_scr[s, pl.ds(tt, BH, stride=PITCH), :] = hn
            out.append(hn)
        return tuple(out)

    carry = lax.fori_loop(0, LT, scan_body, tuple(hc_scr[s] for s in range(NSLAB)), unroll=8)
    for s in range(NSLAB):
        hc_scr[s] = carry[s]

    gr = jnp.dot(hg, win_ref[:, D_LRU:2 * D_LRU], preferred_element_type=F32)
    gg = _gelu(gr)
    for j in range(BH):
        hl = jnp.concatenate([u_scr[s, j * PITCH:j * PITCH + LT, :] for s in range(NSLAB)], axis=1)
        y_scr[j * LT:(j + 1) * LT, 0:D_LRU] = (hl * gg[j * LT:(j + 1) * LT, :]).astype(BF16)

    su = jnp.dot(hg, win_ref[:, 2 * D_LRU:2 * D_LRU + D_SGU], preferred_element_type=F32)
    sv = jnp.dot(hg, win_ref[:, 2 * D_LRU + D_SGU:], preferred_element_type=F32)
    ug = _gelu(su)
    vg = _gelu(sv)
    mu = jnp.mean(vg, axis=-1, keepdims=True)
    vc = vg - mu
    var = jnp.mean(vc * vc, axis=-1, keepdims=True)
    vn = vc * lax.rsqrt(var + EPS) * lng_ref[...] + lnb_ref[...]
    vnt = [vn[j * LT:(j + 1) * LT, :].T for j in range(BH)]
    st = []
    for g in range(HEADS):
        lhs = jnp.concatenate([vnt[j][g * HEAD_DIM:(g + 1) * HEAD_DIM, :] for j in range(BH)], axis=0)
        st.append(jnp.dot(lhs.astype(BF16), sguw_ref[g], preferred_element_type=F32))
    for j in range(BH):
        stj = jnp.concatenate([st[g][j * HEAD_DIM:(j + 1) * HEAD_DIM, :] for g in range(HEADS)], axis=0)
        sj = stj.T + sgub_ref[...]
        y_scr[j * LT:(j + 1) * LT, D_LRU:] = (ug[j * LT:(j + 1) * LT, :] * sj).astype(BF16)

    mix = jnp.dot(y_scr[...], wout_ref[...], preferred_element_type=F32)
    for j in range(BH):
        x1 = x_ref[j] + mod_row(2, j) * mix[j * LT:(j + 1) * LT, :]
        x1_ref[j] = x1
        ms = jnp.mean(x1 * x1, axis=-1, keepdims=True)
        h2 = (x1 * lax.rsqrt(ms + EPS)) * mod_row(4, j) + mod_row(3, j)
        h2_ref[j] = _pack_rows(h2)
        h_scr[j * LT:(j + 1) * LT, :] = h2.astype(BF16)
    lg = jnp.dot(h_scr[...], wr_ref[...], preferred_element_type=F32) + br_ref[...]
    for j in range(BH):
        lg_ref[j] = lg[j * LT:(j + 1) * LT, :]


def _const_spec(shape):
    nd = len(shape)
    return pl.BlockSpec(shape, lambda t, _nd=nd: (0,) * _nd, pipeline_mode=pl.Buffered(1))


def _mixer_call(half, x, mod, win, convw, convb, wg, bg, lam, lng, lnb, sguw, sgub, wout, wr, br):
    out_tile = lambda w: pl.BlockSpec((BH, LT, w), lambda t: (0, t, 0))
    return pl.pallas_call(
        functools.partial(_mixer_kernel, half=half),
        grid=(S // LT,),
        in_specs=[
            pl.BlockSpec((BH, LT, D), lambda t: (half, t, 0)),
            _const_spec((6, B, D)),
            _const_spec((D, 2 * D)),
            _const_spec((CONV_WIDTH, D_LRU)),
            _const_spec((1, D_LRU)),
            _const_spec((D_LRU, 2 * D_LRU)),
            _const_spec((1, 2 * D_LRU)),
            _const_spec((1, D_LRU)),
            _const_spec((1, D_SGU)),
            _const_spec((1, D_SGU)),
            _const_spec((HEADS, SGU_BLOCK, SGU_BLOCK)),
            _const_spec((SGU_BLOCK, D_SGU)),
            _const_spec((D, D)),
            _const_spec((D, LOGIT_W)),
            _const_spec((1, LOGIT_W)),
        ],
        out_specs=[out_tile(D), out_tile(DP), out_tile(LOGIT_W)],
        out_shape=[
            jax.ShapeDtypeStruct((BH, S, D), F32),
            jax.ShapeDtypeStruct((BH, S, DP), I32),
            jax.ShapeDtypeStruct((BH, S, LOGIT_W), F32),
        ],
        scratch_shapes=[
            pltpu.VMEM((ROWS, D), BF16),
            pltpu.VMEM((BH, LT + HALO, D_LRU), F32),
            pltpu.VMEM((NSLAB, BH * PITCH, LANES), F32),
            pltpu.VMEM((NSLAB, BH * PITCH, LANES), F32),
            pltpu.VMEM((NSLAB, BH, LANES), F32),
            pltpu.VMEM((ROWS, D), BF16),
        ],
        compiler_params=pltpu.CompilerParams(
            dimension_semantics=("arbitrary",), vmem_limit_bytes=VMEM_LIMIT),
        name="mixer",
    )(x, mod, win, convw, convb, wg, bg, lam, lng, lnb, sguw, sgub, wout, wr, br)


def _route_kernel(lg_ref, dest_ref, wt_ref, be_ref, cnt_scr, pst_scr, run_scr):
    p = pl.program_id(0)
    i = pl.program_id(1)

    lt = lg_ref[...].T
    row8 = lax.broadcasted_iota(I32, (SUBLANES, TR), 0)
    gl = jnp.where(row8 < N_GROUPS, lt[0:SUBLANES], -jnp.inf)
    gmax = jnp.max(gl, axis=0, keepdims=True)
    gsel = jnp.min(jnp.where(gl == gmax, row8, SUBLANES), axis=0, keepdims=True)
    pg = 1.0 / jnp.sum(jnp.exp(gl - gmax), axis=0, keepdims=True)
    el = lt[SUBLANES:2 * SUBLANES]
    for g in range(1, N_GROUPS):
        el = jnp.where(gsel == g, lt[(g + 1) * SUBLANES:(g + 2) * SUBLANES], el)
    emax = jnp.max(el, axis=0, keepdims=True)
    pe = jnp.exp(el - emax)
    prob = pe / jnp.sum(pe, axis=0, keepdims=True)
    p0 = jnp.max(prob, axis=0, keepdims=True)
    i0 = jnp.min(jnp.where(prob == p0, row8, SUBLANES), axis=0, keepdims=True)
    pm = jnp.where(row8 == i0, -1.0, prob)
    p1 = jnp.max(pm, axis=0, keepdims=True)
    i1 = jnp.min(jnp.where(pm == p1, row8, SUBLANES), axis=0, keepdims=True)
    den = p0 + p1
    w0 = pg * (p0 / den)
    w1 = pg * (p1 / den)
    e0 = gsel * EPG + i0
    e1 = gsel * EPG + i1

    row32 = lax.broadcasted_iota(I32, (N_EXPERTS, TR), 0)
    is0 = row32 == e0
    is1 = row32 == e1
    oh = jnp.where(is0, 1.0, 0.0) + jnp.where(is1, 1.0, 0.0)
    ohsum = jnp.sum(oh, axis=1, keepdims=True)

    @pl.when((p == 0) & (i == 0))
    def _zero():
        cnt_scr[...] = jnp.zeros_like(cnt_scr)

    @pl.when(p == 0)
    def _count():
        cnt_scr[...] = cnt_scr[...] + ohsum

    @pl.when((p == 1) & (i == 0))
    def _starts():
        cnt = cnt_scr[...]
        ntile = jnp.ceil(cnt * (1.0 / TM))
        r = lax.broadcasted_iota(I32, (N_EXPERTS, N_EXPERTS), 0)
        c = lax.broadcasted_iota(I32, (N_EXPERTS, N_EXPERTS), 1)
        lower = jnp.where(c < r, 1.0, 0.0).astype(BF16)
        tstart = jnp.dot(lower, ntile.astype(BF16), preferred_element_type=F32)
        pst_scr[...] = tstart * TM
        run_scr[...] = jnp.zeros_like(run_scr)
        tend = tstart + ntile
        lane = lax.broadcasted_iota(I32, (N_EXPERTS, NBLK_PAD), 1).astype(F32)
        ts1 = tstart[:, 0:1]
        te1 = tend[:, 0:1]
        be = jnp.sum(jnp.where(te1 <= lane, 1.0, 0.0), axis=0, keepdims=True)
        be = jnp.minimum(be, N_EXPERTS - 1.0)
        nused = jnp.sum(ntile[:, 0:1], axis=0, keepdims=True)
        left = jnp.clip(cnt[:, 0:1] - (lane - ts1) * TM, 0.0, TM * 1.0)
        nvalid = jnp.sum(jnp.where((ts1 <= lane) & (lane < te1), left, 0.0), axis=0, keepdims=True)
        rowb = lax.broadcasted_iota(I32, (SUBLANES, NBLK_PAD), 0)
        be_ref[...] = jnp.where(rowb == 0, be, jnp.where(rowb == 1, nused, nvalid)).astype(I32)

    @pl.when(p == 1)
    def _dest():
        rr = lax.broadcasted_iota(I32, (TR, TR), 0)
        cc = lax.broadcasted_iota(I32, (TR, TR), 1)
        upper = jnp.where(rr < cc, 1.0, 0.0).astype(BF16)
        cum = jnp.dot(oh.astype(BF16), upper, preferred_element_type=F32)
        pos = cum + (pst_scr[:, 0:1] + run_scr[:, 0:1])
        d0 = jnp.sum(jnp.where(is0, pos, 0.0), axis=0, keepdims=True)
        d1 = jnp.sum(jnp.where(is1, pos, 0.0), axis=0, keepdims=True)
        run_scr[...] = run_scr[...] + ohsum
        dest_ref[...] = jnp.where(row8 == 0, d0, jnp.where(row8 == 1, d1, 0.0)).astype(I32)
        rowl = lax.broadcasted_iota(I32, (LANES, TR), 0)
        wt_ref[...] = jnp.where(rowl == 0, w0, jnp.where(rowl == 1, w1, 0.0)).T


def _route_call(logits):
    return pl.pallas_call(
        _route_kernel,
        grid=(2, TH // TR),
        in_specs=[pl.BlockSpec((TR, LOGIT_W), lambda p, i: (i, 0))],
        out_specs=[
            pl.BlockSpec((SUBLANES, TR), lambda p, i: (0, i * p)),
            pl.BlockSpec((TR, LANES), lambda p, i: (i * p, 0)),
            pl.BlockSpec((SUBLANES, NBLK_PAD), lambda p, i: (0, 0)),
        ],
        out_shape=[
            jax.ShapeDtypeStruct((SUBLANES, TH), I32),
            jax.ShapeDtypeStruct((TH, LANES), F32),
            jax.ShapeDtypeStruct((SUBLANES, NBLK_PAD), I32),
        ],
        scratch_shapes=[
            pltpu.VMEM((N_EXPERTS, LANES), F32),
            pltpu.VMEM((N_EXPERTS, LANES), F32),
            pltpu.VMEM((N_EXPERTS, LANES), F32),
        ],
        compiler_params=pltpu.CompilerParams(
            dimension_semantics=("arbitrary", "arbitrary"), vmem_limit_bytes=VMEM_LIMIT),
        name="route",
    )(logits)


def _sc_mesh():
    return plsc.VectorSubcoreMesh(core_axis_name="c", subcore_axis_name="s")


def _sc_worker():
    return lax.axis_index("s") * SC_CORES + lax.axis_index("c")


def _dispatch_kernel(h2_hbm, d0_hbm, d1_hbm, xs_hbm, idx0_v, idx1_v, rows_v, sem):
    wid = _sc_worker()
    pltpu.sync_copy(d0_hbm.at[wid], idx0_v)
    pltpu.sync_copy(d1_hbm.at[wid], idx1_v)
    base = wid * SC_TOK
    for j in range(SC_STEPS):
        start = pl.multiple_of(base + j * SC_ROWS, SC_ROWS)
        pltpu.sync_copy(h2_hbm.at[pl.ds(start, SC_ROWS)], rows_v)
        c0 = pltpu.async_copy(rows_v, xs_hbm.at[idx0_v.at[j]], sem)
        c1 = pltpu.async_copy(rows_v, xs_hbm.at[idx1_v.at[j]], sem)
        c0.wait()
        c1.wait()


def _dispatch_call(h2p, d0, d1):
    return pl.kernel(
        _dispatch_kernel,
        out_type=jax.ShapeDtypeStruct((CAP, DP), I32),
        mesh=_sc_mesh(),
        scratch_types=[
            pltpu.VMEM((SC_STEPS, SC_ROWS), I32),
            pltpu.VMEM((SC_STEPS, SC_ROWS), I32),
            pltpu.VMEM((SC_ROWS, DP), I32),
            pltpu.SemaphoreType.DMA,
        ],
        name="dispatch",
    )(h2p, d0, d1)


def _gather_kernel(ys_hbm, d0_hbm, d1_hbm, y0_hbm, y1_hbm, idx0_v, idx1_v, rows_v, sem):
    wid = _sc_worker()
    pltpu.sync_copy(d0_hbm.at[wid], idx0_v)
    pltpu.sync_copy(d1_hbm.at[wid], idx1_v)
    base = wid * SC_TOK
    for j in range(SC_STEPS):
        start = pl.multiple_of(base + j * SC_ROWS, SC_ROWS)
        pltpu.async_copy(ys_hbm.at[idx0_v.at[j]], rows_v, sem).wait()
        pltpu.sync_copy(rows_v, y0_hbm.at[pl.ds(start, SC_ROWS)])
        pltpu.async_copy(ys_hbm.at[idx1_v.at[j]], rows_v, sem).wait()
        pltpu.sync_copy(rows_v, y1_hbm.at[pl.ds(start, SC_ROWS)])


def _gather_call(ys, d0, d1):
    row = jax.ShapeDtypeStruct((TH, DP), I32)
    return pl.kernel(
        _gather_kernel,
        out_type=(row, row),
        mesh=_sc_mesh(),
        scratch_types=[
            pltpu.VMEM((SC_STEPS, SC_ROWS), I32),
            pltpu.VMEM((SC_STEPS, SC_ROWS), I32),
            pltpu.VMEM((SC_ROWS, DP), I32),
            pltpu.SemaphoreType.DMA,
        ],
        name="gather",
    )(ys, d0, d1)


def _experts_kernel(be_ref, nu_ref, nv_ref, xs_ref, w1_ref, w3_ref, w2_ref, ys_ref, w1c, w3c, w2c):
    j = pl.program_id(0)

    @pl.when(j < nu_ref[0])
    def _active():
        @pl.when((j == 0) | (be_ref[j] != be_ref[jnp.maximum(j - 1, 0)]))
        def _cast():
            w1c[...] = w1_ref[0].astype(BF16)
            w3c[...] = w3_ref[0].astype(BF16)
            w2c[...] = w2_ref[0].astype(BF16)

        row = lax.broadcasted_iota(I32, (TM, DP), 0)
        words = jnp.where(row < nv_ref[j], xs_ref[...], 0)
        xb = _unpack_rows(words).astype(BF16)
        h1 = jnp.dot(xb, w1c[...], preferred_element_type=F32)
        h3 = jnp.dot(xb, w3c[...], preferred_element_type=F32)
        act = (h1 * _sigmoid(h1)) * h3
        y = jnp.dot(act.astype(BF16), w2c[...], preferred_element_type=F32)
        ys_ref[...] = _pack_rows(y)


def _experts_call(block_expert, n_used, n_valid, xs, w1, w3, w2):
    def x_map(j, be, nu, nv):
        return (jnp.minimum(j, nu[0] - 1), 0)

    def w_map(j, be, nu, nv):
        return (be[jnp.minimum(j, nu[0] - 1)], 0, 0)

    grid_spec = pltpu.PrefetchScalarGridSpec(
        num_scalar_prefetch=3,
        grid=(NBLK,),
        in_specs=[
            pl.BlockSpec((TM, DP), x_map),
            pl.BlockSpec((1, D, D_EXPERT), w_map),
            pl.BlockSpec((1, D, D_EXPERT), w_map),
            pl.BlockSpec((1, D_EXPERT, D), w_map),
        ],
        out_specs=pl.BlockSpec((TM, DP), x_map),
        scratch_shapes=[
            pltpu.VMEM((D, D_EXPERT), BF16),
            pltpu.VMEM((D, D_EXPERT), BF16),
            pltpu.VMEM((D_EXPERT, D), BF16),
        ],
    )
    return pl.pallas_call(
        _experts_kernel,
        grid_spec=grid_spec,
        out_shape=jax.ShapeDtypeStruct((CAP, DP), I32),
        compiler_params=pltpu.CompilerParams(
            dimension_semantics=("arbitrary",), vmem_limit_bytes=VMEM_LIMIT),
        name="experts",
    )(block_expert, n_used, n_valid, xs, w1, w3, w2)


def _combine_kernel(x1_ref, g2_ref, wt_ref, fg_ref, y0_ref, y1_ref, *rest):
    o_ref = rest[-1]
    wt = wt_ref[...]
    y = wt[:, 0:1] * _unpack_rows(y0_ref[...]) + wt[:, 1:2] * _unpack_rows(y1_ref[...])
    x2 = x1_ref[...] + g2_ref[0] * y
    ms = jnp.mean(x2 * x2, axis=-1, keepdims=True)
    o_ref[...] = (x2 * lax.rsqrt(ms + EPS)) * fg_ref[...]


def _combine_call(half, prev_out, x1, mod, wt, final_g, y0, y1):
    per_b = S // TKC
    nsteps = TH // TKC
    in_specs = [
        pl.BlockSpec((TKC, D), lambda i: (i, 0)),
        pl.BlockSpec((None, 1, 1, D), lambda i: (5, half * BH + i // per_b, 0, 0)),
        pl.BlockSpec((TKC, LANES), lambda i: (i, 0)),
        pl.BlockSpec((1, D), lambda i: (0, 0)),
        pl.BlockSpec((TKC, DP), lambda i: (i, 0)),
        pl.BlockSpec((TKC, DP), lambda i: (i, 0)),
    ]
    args = [x1, mod.reshape(6, B, 1, D), wt, final_g, y0, y1]
    aliases = {}
    if prev_out is not None:
        in_specs.append(pl.BlockSpec(memory_space=pl.ANY))
        args.append(prev_out)
        aliases = {len(args) - 1: 0}
    return pl.pallas_call(
        _combine_kernel,
        grid=(nsteps,),
        in_specs=in_specs,
        out_specs=pl.BlockSpec((TKC, D), lambda i: (half * nsteps + i, 0)),
        out_shape=jax.ShapeDtypeStruct((T, D), F32),
        input_output_aliases=aliases,
        compiler_params=pltpu.CompilerParams(
            dimension_semantics=("arbitrary",), vmem_limit_bytes=VMEM_LIMIT),
        name="combine",
    )(*args)


def _block_diag(w):
    eye = jnp.eye(HEADS, dtype=w.dtype)
    return jnp.einsum("hde,hg->hdge", w, eye).reshape(D_LRU, D_LRU)


def kernel(x, c, ada_w, ada_b, norm1_g, w_in, conv_w, conv_b, gate_a_w, gate_a_b, gate_i_w, gate_i_b, lru_lambda, sgu_ln_g, sgu_ln_b, sgu_w, sgu_b, w_out, norm2_g, router_group_w, router_group_b, router_expert_w, router_expert_b, expert_w1, expert_w3, expert_w2, final_g):
    l = 0
    ones = jnp.ones((D,), F32)
    zeros = jnp.zeros((D,), F32)
    mul = jnp.stack([ones, norm1_g[l], ones, ones, norm2_g[l], ones]).reshape(6, 1, D)
    add = jnp.stack([zeros, norm1_g[l], zeros, zeros, norm2_g[l], zeros]).reshape(6, 1, D)
    mod = _mod_call(c, ada_w[l], ada_b[l], mul, add)

    pos_chunk = jnp.arange(SGU_BLOCK) // CHUNK
    mask = (pos_chunk[None, :] <= pos_chunk[:, None]).astype(F32)
    sguw_t = jnp.transpose(sgu_w[l] * mask[None], (0, 2, 1)).astype(BF16)
    sgub_tile = jnp.repeat(sgu_b[l].T, HEAD_DIM, axis=1)
    wg = jnp.concatenate([_block_diag(gate_a_w[l]), _block_diag(gate_i_w[l])], axis=1).astype(BF16)
    bg = jnp.concatenate([gate_a_b[l], gate_i_b[l]]).reshape(1, 2 * D_LRU)
    wr = jnp.zeros((D, LOGIT_W), F32)
    wr = wr.at[:, 0:N_GROUPS].set(router_group_w[l]).at[:, SUBLANES:SUBLANES + N_EXPERTS].set(router_expert_w[l])
    br = jnp.zeros((1, LOGIT_W), F32)
    br = br.at[0, 0:N_GROUPS].set(router_group_b[l]).at[0, SUBLANES:SUBLANES + N_EXPERTS].set(router_expert_b[l])
    mixer_params = (
        w_in[l].astype(BF16), conv_w[l], conv_b[l].reshape(1, D_LRU), wg, bg,
        lru_lambda[l].reshape(1, D_LRU), sgu_ln_g[l].reshape(1, D_SGU), sgu_ln_b[l].reshape(1, D_SGU),
        sguw_t, sgub_tile, w_out[l].astype(BF16), wr.astype(BF16), br)

    staged = []
    for half in range(NH):
        x1, h2p, logits = _mixer_call(half, x, mod, *mixer_params)
        dest, wt, be = _route_call(logits.reshape(TH, LOGIT_W))
        d0 = dest[0].reshape(SC_WORKERS, SC_STEPS, SC_ROWS)
        d1 = dest[1].reshape(SC_WORKERS, SC_STEPS, SC_ROWS)
        xs = _dispatch_call(h2p.reshape(TH, DP), d0, d1)
        ys = _experts_call(be[0, 0:NBLK], be[1, 0:1], be[2, 0:NBLK], xs,
                           expert_w1[l], expert_w3[l], expert_w2[l])
        y0, y1 = _gather_call(ys, d0, d1)
        staged.append((x1.reshape(TH, D), wt, y0, y1))

    out = None
    for half, (x1, wt, y0, y1) in enumerate(staged):
        out = _combine_call(half, out, x1, mod, wt, final_g.reshape(1, D), y0, y1)
    return out.reshape(B, S, D)
```

```python
import functools

import jax
import jax.numpy as jnp
from jax import lax
from jax.experimental import pallas as pl
from jax.experimental.pallas import tpu as pltpu
from jax.experimental.pallas import tpu_sc as plsc

F32 = jnp.float32
BF16 = jnp.bfloat16
I32 = jnp.int32
U32 = jnp.uint32

D = 1024
B = 8
S = 4096
T = B * S
D_LRU = 512
D_SGU = 512
HEADS = 8
HEAD_DIM = 64
CONV_WIDTH = 4
LRU_C = 8.0
SGU_BLOCK = 128
CHUNK = 64
N_GROUPS = 4
EPG = 8
N_EXPERTS = N_GROUPS * EPG
TOP_K = 2
D_EXPERT = 512
EPS = 1e-6
DP = D // 2

LANES = 128
SUBLANES = 8
VMEM_LIMIT = 60 * 1024 * 1024

NH = 2
BH = B // NH
TH = BH * S

SC_CORES = 2
SC_SUBCORES = 16
SC_WORKERS = SC_CORES * SC_SUBCORES
SC_ROWS = 64
SC_TOK = TH // SC_WORKERS
SC_STEPS = SC_TOK // SC_ROWS

LT = SGU_BLOCK
ROWS = BH * LT
PITCH = LT + SUBLANES
HALO = SUBLANES
NSLAB = D_LRU // LANES

TM = 512
N_SLOTS = TH * TOP_K
CAP = N_SLOTS + N_EXPERTS * TM
NBLK = CAP // TM
NBLK_PAD = LANES
TR = 1024
TKC = 512
LOGIT_W = LANES


_GELU_A = -2.0 * 0.7978845608028654 * 1.4426950408889634
_GELU_B = _GELU_A * 0.044715


def _gelu(x):
    return x / (1.0 + jnp.exp2(x * (_GELU_A + _GELU_B * (x * x))))


def _sigmoid(x):
    return 1.0 / (1.0 + jnp.exp(-x))


def _pack_rows(x):
    lo = lax.bitcast_convert_type(x[:, 0:DP].astype(BF16).astype(F32), U32)
    hi = lax.bitcast_convert_type(x[:, DP:D].astype(BF16).astype(F32), U32)
    word = hi | (lo >> 16)
    return lax.bitcast_convert_type(word, I32)


def _unpack_rows(w):
    u = lax.bitcast_convert_type(w, U32)
    lo = lax.bitcast_convert_type(u << 16, F32)
    hi = lax.bitcast_convert_type(u & jnp.uint32(0xFFFF0000), F32)
    return jnp.concatenate([lo, hi], axis=1)


def _mod_kernel(c_ref, w_ref, b_ref, mul_ref, add_ref, o_ref):
    c = c_ref[...]
    cond = c * _sigmoid(c)
    m = jnp.dot(cond, w_ref[...], preferred_element_type=F32, precision=lax.Precision.HIGHEST)
    m = m + b_ref[0]
    o_ref[0] = m * mul_ref[0] + add_ref[0]


def _mod_call(c, ada_w, ada_b, mul, add):
    return pl.pallas_call(
        _mod_kernel,
        grid=(6,),
        in_specs=[
            pl.BlockSpec((B, D), lambda k: (0, 0)),
            pl.BlockSpec((D, D), lambda k: (0, k)),
            pl.BlockSpec((1, 1, D), lambda k: (k, 0, 0)),
            pl.BlockSpec((1, 1, D), lambda k: (k, 0, 0)),
            pl.BlockSpec((1, 1, D), lambda k: (k, 0, 0)),
        ],
        out_specs=pl.BlockSpec((1, B, D), lambda k: (k, 0, 0)),
        out_shape=jax.ShapeDtypeStruct((6, B, D), F32),
        name="mod",
    )(c, ada_w, ada_b.reshape(6, 1, D), mul, add)


N_MIXER_IN = 18


def _mixer_kernel(*refs, half):
    (x_ref, mod_ref, win_ref, convw_ref, convb_ref, wg_ref, bg_ref, lam_ref,
     lng_ref, lnb_ref, sguw_ref, sgub_ref, wout_ref, wr_ref, br_ref,
     e1_ref, e3_ref, e2_ref) = refs[:N_MIXER_IN]
    (x1_ref, h2_ref, lg_ref, e1b_ref, e3b_ref, e2b_ref,
     h_scr, xr_scr, a_scr, u_scr, hc_scr, y_scr) = refs[-12:]
    t = pl.program_id(0)

    e1b_ref[...] = e1_ref[...].astype(BF16)
    e3b_ref[...] = e3_ref[...].astype(BF16)
    e2b_ref[...] = e2_ref[...].astype(BF16)

    @pl.when(t == 0)
    def _init():
        hc_scr[...] = jnp.zeros_like(hc_scr)
        xr_scr[:, 0:HALO, :] = jnp.zeros((BH, HALO, D_LRU), F32)

    def mod_row(k, j):
        b = half * BH + j
        return mod_ref[k, b:b + 1, :]

    lam = lam_ref[...]
    z = -lam
    softplus = jnp.maximum(z, 0.0) + jnp.log1p(jnp.exp(-jnp.abs(z)))
    clam = -LRU_C * softplus
    cw = convw_ref[...]
    cb = convb_ref[...]

    for j in range(BH):
        xb = x_ref[j]
        ms = jnp.mean(xb * xb, axis=-1, keepdims=True)
        hb = (xb * lax.rsqrt(ms + EPS)) * mod_row(1, j) + mod_row(0, j)
        h_scr[j * LT:(j + 1) * LT, :] = hb.astype(BF16)
    hg = h_scr[...]
    xr = jnp.dot(hg, win_ref[:, 0:D_LRU], preferred_element_type=F32)
    xcs = []
    for j in range(BH):
        xr_scr[j, HALO:HALO + LT, :] = xr[j * LT:(j + 1) * LT, :]
        acc = cb + cw[CONV_WIDTH - 1:CONV_WIDTH] * xr_scr[j, HALO:HALO + LT, :]
        for k in range(CONV_WIDTH - 1):
            off = HALO - (CONV_WIDTH - 1) + k
            acc = acc + cw[k:k + 1] * xr_scr[j, off:off + LT, :]
        xcs.append(acc)
        xr_scr[j, 0:HALO, :] = xr_scr[j, LT:LT + HALO, :]
    xc = jnp.concatenate(xcs, axis=0)
    gpre = jnp.dot(xc.astype(BF16), wg_ref[...], preferred_element_type=F32) + bg_ref[...]
    r = _sigmoid(gpre[:, 0:D_LRU])
    gi = _sigmoid(gpre[:, D_LRU:2 * D_LRU])
    a = jnp.exp(clam * r)
    uin = jnp.sqrt(1.0 - a * a) * (gi * xc)
    for j in range(BH):
        for s in range(NSLAB):
            a_scr[s, j * PITCH:j * PITCH + LT, :] = a[j * LT:(j + 1) * LT, s * LANES:(s + 1) * LANES]
            u_scr[s, j * PITCH:j * PITCH + LT, :] = uin[j * LT:(j + 1) * LT, s * LANES:(s + 1) * LANES]

    def scan_body(tt, carry):
        out = []
        for s in range(NSLAB):
            at = a_scr[s, pl.ds(tt, BH, stride=PITCH), :]
            ut = u_scr[s, pl.ds(tt, BH, stride=PITCH), :]
            hn = at * carry[s] + ut
            u_scr[s, pl.ds(tt, BH, stride=PITCH), :] = hn
            out.append(hn)
        return tuple(out)

    carry = lax.fori_loop(0, LT, scan_body, tuple(hc_scr[s] for s in range(NSLAB)), unroll=8)
    for s in range(NSLAB):
        hc_scr[s] = carry[s]

    gr = jnp.dot(hg, win_ref[:, D_LRU:2 * D_LRU], preferred_element_type=F32)
    gg = _gelu(gr)
    for j in range(BH):
        hl = jnp.concatenate([u_scr[s, j * PITCH:j * PITCH + LT, :] for s in range(NSLAB)], axis=1)
        y_scr[j * LT:(j + 1) * LT, 0:D_LRU] = (hl * gg[j * LT:(j + 1) * LT, :]).astype(BF16)

    su = jnp.dot(hg, win_ref[:, 2 * D_LRU:2 * D_LRU + D_SGU], preferred_element_type=F32)
    sv = jnp.dot(hg, win_ref[:, 2 * D_LRU + D_SGU:], preferred_element_type=F32)
    ug = _gelu(su)
    vg = _gelu(sv)
    mu = jnp.mean(vg, axis=-1, keepdims=True)
    vc = vg - mu
    var = jnp.mean(vc * vc, axis=-1, keepdims=True)
    vn = vc * lax.rsqrt(var + EPS) * lng_ref[...] + lnb_ref[...]
    vnt = [vn[j * LT:(j + 1) * LT, :].T for j in range(BH)]
    st = []
    for g in range(HEADS):
        lhs = jnp.concatenate([vnt[j][g * HEAD_DIM:(g + 1) * HEAD_DIM, :] for j in range(BH)], axis=0)
        st.append(jnp.dot(lhs.astype(BF16), sguw_ref[g], preferred_element_type=F32))
    for j in range(BH):
        stj = jnp.concatenate([st[g][j * HEAD_DIM:(j + 1) * HEAD_DIM, :] for g in range(HEADS)], axis=0)
        sj = stj.T + sgub_ref[...]
        y_scr[j * LT:(j + 1) * LT, D_LRU:] = (ug[j * LT:(j + 1) * LT, :] * sj).astype(BF16)

    mix = jnp.dot(y_scr[...], wout_ref[...], preferred_element_type=F32)
    for j in range(BH):
        x1 = x_ref[j] + mod_row(2, j) * mix[j * LT:(j + 1) * LT, :]
        x1_ref[j] = x1
        ms = jnp.mean(x1 * x1, axis=-1, keepdims=True)
        h2 = (x1 * lax.rsqrt(ms + EPS)) * mod_row(4, j) + mod_row(3, j)
        h2_ref[j] = _pack_rows(h2)
        h_scr[j * LT:(j + 1) * LT, :] = h2.astype(BF16)
    lg = jnp.dot(h_scr[...], wr_ref[...], preferred_element_type=F32) + br_ref[...]
    for j in range(BH):
        lg_ref[j] = lg[j * LT:(j + 1) * LT, :]


def _const_spec(shape):
    nd = len(shape)
    return pl.BlockSpec(shape, lambda t, _nd=nd: (0,) * _nd, pipeline_mode=pl.Buffered(1))


STEPS = S // LT
E_PARTS = NH * STEPS // N_EXPERTS
E_PER_HALF = N_EXPERTS // NH


def _mixer_call(half, anchor, prev_bf16, x, mod, win, convw, convb, wg, bg, lam, lng, lnb, sguw, sgub,
                wout, wr, br, e1, e3, e2):
    out_tile = lambda w: pl.BlockSpec((BH, LT, w), lambda t: (0, t, 0))
    e_map = lambda t: (half * E_PER_HALF + t // E_PARTS, t % E_PARTS, 0)
    e13_spec = pl.BlockSpec((1, D // E_PARTS, D_EXPERT), e_map)
    e2_spec = pl.BlockSpec((1, D_EXPERT // E_PARTS, D), e_map)
    extra_specs = [pl.BlockSpec(memory_space=pl.ANY)]
    extra_args = [anchor]
    aliases = {}
    if prev_bf16 is not None:
        for k, buf in enumerate(prev_bf16):
            aliases[N_MIXER_IN + 1 + k] = 3 + k
            extra_specs.append(pl.BlockSpec(memory_space=pl.ANY))
            extra_args.append(buf)
    return pl.pallas_call(
        functools.partial(_mixer_kernel, half=half),
        grid=(STEPS,),
        in_specs=[
            pl.BlockSpec((BH, LT, D), lambda t: (half, t, 0)),
            _const_spec((6, B, D)),
            _const_spec((D, 2 * D)),
            _const_spec((CONV_WIDTH, D_LRU)),
            _const_spec((1, D_LRU)),
            _const_spec((D_LRU, 2 * D_LRU)),
            _const_spec((1, 2 * D_LRU)),
            _const_spec((1, D_LRU)),
            _const_spec((1, D_SGU)),
            _const_spec((1, D_SGU)),
            _const_spec((HEADS, SGU_BLOCK, SGU_BLOCK)),
            _const_spec((SGU_BLOCK, D_SGU)),
            _const_spec((D, D)),
            _const_spec((D, LOGIT_W)),
            _const_spec((1, LOGIT_W)),
            e13_spec, e13_spec, e2_spec,
        ] + extra_specs,
        out_specs=[out_tile(D), out_tile(DP), out_tile(LOGIT_W), e13_spec, e13_spec, e2_spec],
        out_shape=[
            jax.ShapeDtypeStruct((BH, S, D), F32),
            jax.ShapeDtypeStruct((BH, S, DP), I32),
            jax.ShapeDtypeStruct((BH, S, LOGIT_W), F32),
            jax.ShapeDtypeStruct((N_EXPERTS, D, D_EXPERT), BF16),
            jax.ShapeDtypeStruct((N_EXPERTS, D, D_EXPERT), BF16),
            jax.ShapeDtypeStruct((N_EXPERTS, D_EXPERT, D), BF16),
        ],
        input_output_aliases=aliases,
        scratch_shapes=[
            pltpu.VMEM((ROWS, D), BF16),
            pltpu.VMEM((BH, LT + HALO, D_LRU), F32),
            pltpu.VMEM((NSLAB, BH * PITCH, LANES), F32),
            pltpu.VMEM((NSLAB, BH * PITCH, LANES), F32),
            pltpu.VMEM((NSLAB, BH, LANES), F32),
            pltpu.VMEM((ROWS, D), BF16),
        ],
        compiler_params=pltpu.CompilerParams(
            dimension_semantics=("arbitrary",), vmem_limit_bytes=VMEM_LIMIT),
        name="mixer",
    )(x, mod, win, convw, convb, wg, bg, lam, lng, lnb, sguw, sgub, wout, wr, br, e1, e3, e2, *extra_args)


def _route_kernel(lg_ref, dest_ref, wt_ref, be_ref, cnt_scr, pst_scr, run_scr):
    p = pl.program_id(0)
    i = pl.program_id(1)

    lt = lg_ref[...].T
    row8 = lax.broadcasted_iota(I32, (SUBLANES, TR), 0)
    gl = jnp.where(row8 < N_GROUPS, lt[0:SUBLANES], -jnp.inf)
    gmax = jnp.max(gl, axis=0, keepdims=True)
    gsel = jnp.min(jnp.where(gl == gmax, row8, SUBLANES), axis=0, keepdims=True)
    pg = 1.0 / jnp.sum(jnp.exp(gl - gmax), axis=0, keepdims=True)
    el = lt[SUBLANES:2 * SUBLANES]
    for g in range(1, N_GROUPS):
        el = jnp.where(gsel == g, lt[(g + 1) * SUBLANES:(g + 2) * SUBLANES], el)
    emax = jnp.max(el, axis=0, keepdims=True)
    pe = jnp.exp(el - emax)
    prob = pe / jnp.sum(pe, axis=0, keepdims=True)
    p0 = jnp.max(prob, axis=0, keepdims=True)
    i0 = jnp.min(jnp.where(prob == p0, row8, SUBLANES), axis=0, keepdims=True)
    pm = jnp.where(row8 == i0, -1.0, prob)
    p1 = jnp.max(pm, axis=0, keepdims=True)
    i1 = jnp.min(jnp.where(pm == p1, row8, SUBLANES), axis=0, keepdims=True)
    den = p0 + p1
    w0 = pg * (p0 / den)
    w1 = pg * (p1 / den)
    e0 = gsel * EPG + i0
    e1 = gsel * EPG + i1

    row32 = lax.broadcasted_iota(I32, (N_EXPERTS, TR), 0)
    is0 = row32 == e0
    is1 = row32 == e1
    oh = jnp.where(is0, 1.0, 0.0) + jnp.where(is1, 1.0, 0.0)
    ohsum = jnp.sum(oh, axis=1, keepdims=True)

    @pl.when((p == 0) & (i == 0))
    def _zero():
        cnt_scr[...] = jnp.zeros_like(cnt_scr)

    @pl.when(p == 0)
    def _count():
        cnt_scr[...] = cnt_scr[...] + ohsum

    @pl.when((p == 1) & (i == 0))
    def _starts():
        cnt = cnt_scr[...]
        ntile = jnp.ceil(cnt * (1.0 / TM))
        r = lax.broadcasted_iota(I32, (N_EXPERTS, N_EXPERTS), 0)
        c = lax.broadcasted_iota(I32, (N_EXPERTS, N_EXPERTS), 1)
        lower = jnp.where(c < r, 1.0, 0.0).astype(BF16)
        tstart = jnp.dot(lower, ntile.astype(BF16), preferred_element_type=F32)
        pst_scr[...] = tstart * TM
        run_scr[...] = jnp.zeros_like(run_scr)
        tend = tstart + ntile
        lane = lax.broadcasted_iota(I32, (N_EXPERTS, NBLK_PAD), 1).astype(F32)
        ts1 = tstart[:, 0:1]
        te1 = tend[:, 0:1]
        be = jnp.sum(jnp.where(te1 <= lane, 1.0, 0.0), axis=0, keepdims=True)
        be = jnp.minimum(be, N_EXPERTS - 1.0)
        nused = jnp.sum(ntile[:, 0:1], axis=0, keepdims=True)
        left = jnp.clip(cnt[:, 0:1] - (lane - ts1) * TM, 0.0, TM * 1.0)
        nvalid = jnp.sum(jnp.where((ts1 <= lane) & (lane < te1), left, 0.0), axis=0, keepdims=True)
        rowb = lax.broadcasted_iota(I32, (SUBLANES, NBLK_PAD), 0)
        be_ref[...] = jnp.where(rowb == 0, be, jnp.where(rowb == 1, nused, nvalid)).astype(I32)

    @pl.when(p == 1)
    def _dest():
        rr = lax.broadcasted_iota(I32, (TR, TR), 0)
        cc = lax.broadcasted_iota(I32, (TR, TR), 1)
        upper = jnp.where(rr < cc, 1.0, 0.0).astype(BF16)
        cum = jnp.dot(oh.astype(BF16), upper, preferred_element_type=F32)
        pos = cum + (pst_scr[:, 0:1] + run_scr[:, 0:1])
        d0 = jnp.sum(jnp.where(is0, pos, 0.0), axis=0, keepdims=True)
        d1 = jnp.sum(jnp.where(is1, pos, 0.0), axis=0, keepdims=True)
        run_scr[...] = run_scr[...] + ohsum
        dest_ref[...] = jnp.where(row8 == 0, d0, jnp.where(row8 == 1, d1, 0.0)).astype(I32)
        rowl = lax.broadcasted_iota(I32, (LANES, TR), 0)
        wt_ref[...] = jnp.where(rowl == 0, w0, jnp.where(rowl == 1, w1, 0.0)).T


def _route_call(logits):
    return pl.pallas_call(
        _route_kernel,
        grid=(2, TH // TR),
        in_specs=[pl.BlockSpec((TR, LOGIT_W), lambda p, i: (i, 0))],
        out_specs=[
            pl.BlockSpec((SUBLANES, TR), lambda p, i: (0, i * p)),
            pl.BlockSpec((TR, LANES), lambda p, i: (i * p, 0)),
            pl.BlockSpec((SUBLANES, NBLK_PAD), lambda p, i: (0, 0)),
        ],
        out_shape=[
            jax.ShapeDtypeStruct((SUBLANES, TH), I32),
            jax.ShapeDtypeStruct((TH, LANES), F32),
            jax.ShapeDtypeStruct((SUBLANES, NBLK_PAD), I32),
        ],
        scratch_shapes=[
            pltpu.VMEM((N_EXPERTS, LANES), F32),
            pltpu.VMEM((N_EXPERTS, LANES), F32),
            pltpu.VMEM((N_EXPERTS, LANES), F32),
        ],
        compiler_params=pltpu.CompilerParams(
            dimension_semantics=("arbitrary", "arbitrary"), vmem_limit_bytes=VMEM_LIMIT),
        name="route",
    )(logits)


def _sc_mesh():
    return plsc.VectorSubcoreMesh(core_axis_name="c", subcore_axis_name="s")


def _sc_worker():
    return lax.axis_index("s") * SC_CORES + lax.axis_index("c")


def _dispatch_kernel(h2_hbm, d0_hbm, d1_hbm, xs_hbm, idx0_v, idx1_v, rows_v, sem):
    wid = _sc_worker()
    pltpu.sync_copy(d0_hbm.at[wid], idx0_v)
    pltpu.sync_copy(d1_hbm.at[wid], idx1_v)
    base = wid * SC_TOK
    for j in range(SC_STEPS):
        start = pl.multiple_of(base + j * SC_ROWS, SC_ROWS)
        pltpu.sync_copy(h2_hbm.at[pl.ds(start, SC_ROWS)], rows_v)
        c0 = pltpu.async_copy(rows_v, xs_hbm.at[idx0_v.at[j]], sem)
        c1 = pltpu.async_copy(rows_v, xs_hbm.at[idx1_v.at[j]], sem)
        c0.wait()
        c1.wait()


def _dispatch_call(h2p, d0, d1):
    return pl.kernel(
        _dispatch_kernel,
        out_type=jax.ShapeDtypeStruct((CAP, DP), I32),
        mesh=_sc_mesh(),
        scratch_types=[
            pltpu.VMEM((SC_STEPS, SC_ROWS), I32),
            pltpu.VMEM((SC_STEPS, SC_ROWS), I32),
            pltpu.VMEM((SC_ROWS, DP), I32),
            pltpu.SemaphoreType.DMA,
        ],
        name="dispatch",
    )(h2p, d0, d1)


def _gather_kernel(ys_hbm, d0_hbm, d1_hbm, y0_hbm, y1_hbm, idx0_v, idx1_v, rows_v, sem):
    wid = _sc_worker()
    pltpu.sync_copy(d0_hbm.at[wid], idx0_v)
    pltpu.sync_copy(d1_hbm.at[wid], idx1_v)
    base = wid * SC_TOK
    for j in range(SC_STEPS):
        start = pl.multiple_of(base + j * SC_ROWS, SC_ROWS)
        pltpu.async_copy(ys_hbm.at[idx0_v.at[j]], rows_v, sem).wait()
        pltpu.sync_copy(rows_v, y0_hbm.at[pl.ds(start, SC_ROWS)])
        pltpu.async_copy(ys_hbm.at[idx1_v.at[j]], rows_v, sem).wait()
        pltpu.sync_copy(rows_v, y1_hbm.at[pl.ds(start, SC_ROWS)])


def _gather_call(ys, d0, d1):
    row = jax.ShapeDtypeStruct((TH, DP), I32)
    return pl.kernel(
        _gather_kernel,
        out_type=(row, row),
        mesh=_sc_mesh(),
        scratch_types=[
            pltpu.VMEM((SC_STEPS, SC_ROWS), I32),
            pltpu.VMEM((SC_STEPS, SC_ROWS), I32),
            pltpu.VMEM((SC_ROWS, DP), I32),
            pltpu.SemaphoreType.DMA,
        ],
        name="gather",
    )(ys, d0, d1)


def _experts_kernel(be_ref, nu_ref, nv_ref, xs_ref, w1_ref, w3_ref, w2_ref, ys_ref):
    j = pl.program_id(0)

    @pl.when(j < nu_ref[0])
    def _active():
        row = lax.broadcasted_iota(I32, (TM, DP), 0)
        words = jnp.where(row < nv_ref[j], xs_ref[...], 0)
        xb = _unpack_rows(words).astype(BF16)
        h1 = jnp.dot(xb, w1_ref[0], preferred_element_type=F32)
        h3 = jnp.dot(xb, w3_ref[0], preferred_element_type=F32)
        act = (h1 * _sigmoid(h1)) * h3
        y = jnp.dot(act.astype(BF16), w2_ref[0], preferred_element_type=F32)
        ys_ref[...] = _pack_rows(y)


def _experts_call(block_expert, n_used, n_valid, xs, w1, w3, w2):
    def x_map(j, be, nu, nv):
        return (jnp.minimum(j, nu[0] - 1), 0)

    def w_map(j, be, nu, nv):
        return (be[jnp.minimum(j, nu[0] - 1)], 0, 0)

    grid_spec = pltpu.PrefetchScalarGridSpec(
        num_scalar_prefetch=3,
        grid=(NBLK,),
        in_specs=[
            pl.BlockSpec((TM, DP), x_map),
            pl.BlockSpec((1, D, D_EXPERT), w_map),
            pl.BlockSpec((1, D, D_EXPERT), w_map),
            pl.BlockSpec((1, D_EXPERT, D), w_map),
        ],
        out_specs=pl.BlockSpec((TM, DP), x_map),
    )
    return pl.pallas_call(
        _experts_kernel,
        grid_spec=grid_spec,
        out_shape=jax.ShapeDtypeStruct((CAP, DP), I32),
        compiler_params=pltpu.CompilerParams(
            dimension_semantics=("arbitrary",), vmem_limit_bytes=VMEM_LIMIT),
        name="experts",
    )(block_expert, n_used, n_valid, xs, w1, w3, w2)


def _combine_kernel(x1_ref, g2_ref, wt_ref, fg_ref, y0_ref, y1_ref, *rest):
    o_ref = rest[-1]
    wt = wt_ref[...]
    y = wt[:, 0:1] * _unpack_rows(y0_ref[...]) + wt[:, 1:2] * _unpack_rows(y1_ref[...])
    x2 = x1_ref[...] + g2_ref[0] * y
    ms = jnp.mean(x2 * x2, axis=-1, keepdims=True)
    o_ref[...] = (x2 * lax.rsqrt(ms + EPS)) * fg_ref[...]


def _combine_call(half, prev_out, x1, mod, wt, final_g, y0, y1):
    per_b = S // TKC
    nsteps = TH // TKC
    in_specs = [
        pl.BlockSpec((TKC, D), lambda i: (i, 0)),
        pl.BlockSpec((None, 1, 1, D), lambda i: (5, half * BH + i // per_b, 0, 0)),
        pl.BlockSpec((TKC, LANES), lambda i: (i, 0)),
        pl.BlockSpec((1, D), lambda i: (0, 0)),
        pl.BlockSpec((TKC, DP), lambda i: (i, 0)),
        pl.BlockSpec((TKC, DP), lambda i: (i, 0)),
    ]
    args = [x1, mod.reshape(6, B, 1, D), wt, final_g, y0, y1]
    aliases = {}
    if prev_out is not None:
        in_specs.append(pl.BlockSpec(memory_space=pl.ANY))
        args.append(prev_out)
        aliases = {len(args) - 1: 0}
    return pl.pallas_call(
        _combine_kernel,
        grid=(nsteps,),
        in_specs=in_specs,
        out_specs=pl.BlockSpec((TKC, D), lambda i: (half * nsteps + i, 0)),
        out_shape=jax.ShapeDtypeStruct((T, D), F32),
        input_output_aliases=aliases,
        compiler_params=pltpu.CompilerParams(
            dimension_semantics=("arbitrary",), vmem_limit_bytes=VMEM_LIMIT),
        name="combine",
    )(*args)


def _block_diag(w):
    eye = jnp.eye(HEADS, dtype=w.dtype)
    return jnp.einsum("hde,hg->hdge", w, eye).reshape(D_LRU, D_LRU)


def kernel(x, c, ada_w, ada_b, norm1_g, w_in, conv_w, conv_b, gate_a_w, gate_a_b, gate_i_w, gate_i_b, lru_lambda, sgu_ln_g, sgu_ln_b, sgu_w, sgu_b, w_out, norm2_g, router_group_w, router_group_b, router_expert_w, router_expert_b, expert_w1, expert_w3, expert_w2, final_g):
    l = 0
    ones = jnp.ones((D,), F32)
    zeros = jnp.zeros((D,), F32)
    mul = jnp.stack([ones, norm1_g[l], ones, ones, norm2_g[l], ones]).reshape(6, 1, D)
    add = jnp.stack([zeros, norm1_g[l], zeros, zeros, norm2_g[l], zeros]).reshape(6, 1, D)
    mod = _mod_call(c, ada_w[l], ada_b[l], mul, add)

    pos_chunk = jnp.arange(SGU_BLOCK) // CHUNK
    mask = (pos_chunk[None, :] <= pos_chunk[:, None]).astype(F32)
    sguw_t = jnp.transpose(sgu_w[l] * mask[None], (0, 2, 1)).astype(BF16)
    sgub_tile = jnp.repeat(sgu_b[l].T, HEAD_DIM, axis=1)
    wg = jnp.concatenate([_block_diag(gate_a_w[l]), _block_diag(gate_i_w[l])], axis=1).astype(BF16)
    bg = jnp.concatenate([gate_a_b[l], gate_i_b[l]]).reshape(1, 2 * D_LRU)
    wr = jnp.zeros((D, LOGIT_W), F32)
    wr = wr.at[:, 0:N_GROUPS].set(router_group_w[l]).at[:, SUBLANES:SUBLANES + N_EXPERTS].set(router_expert_w[l])
    br = jnp.zeros((1, LOGIT_W), F32)
    br = br.at[0, 0:N_GROUPS].set(router_group_b[l]).at[0, SUBLANES:SUBLANES + N_EXPERTS].set(router_expert_b[l])
    mixer_params = (
        w_in[l].astype(BF16), conv_w[l], conv_b[l].reshape(1, D_LRU), wg, bg,
        lru_lambda[l].reshape(1, D_LRU), sgu_ln_g[l].reshape(1, D_SGU), sgu_ln_b[l].reshape(1, D_SGU),
        sguw_t, sgub_tile, w_out[l].astype(BF16), wr.astype(BF16), br)

    routed = []
    anchor = jnp.zeros((SUBLANES, NBLK_PAD), I32)
    ebf16 = None
    for half in range(NH):
        x1, h2p, logits, e1b, e3b, e2b = _mixer_call(
            half, anchor, ebf16, x, mod, *mixer_params, expert_w1[l], expert_w3[l], expert_w2[l])
        ebf16 = (e1b, e3b, e2b)
        dest, wt, be = _route_call(logits.reshape(TH, LOGIT_W))
        anchor = be
        d0 = dest[0].reshape(SC_WORKERS, SC_STEPS, SC_ROWS)
        d1 = dest[1].reshape(SC_WORKERS, SC_STEPS, SC_ROWS)
        xs = _dispatch_call(h2p.reshape(TH, DP), d0, d1)
        routed.append((x1.reshape(TH, D), wt, be, d0, d1, xs))

    staged = []
    for x1, wt, be, d0, d1, xs in routed:
        ys = _experts_call(be[0, 0:NBLK], be[1, 0:1], be[2, 0:NBLK], xs, *ebf16)
        y0, y1 = _gather_call(ys, d0, d1)
        staged.append((x1, wt, y0, y1))

    out = None
    for half, (x1, wt, y0, y1) in enumerate(staged):
        out = _combine_call(half, out, x1, mod, wt, final_g.reshape(1, D), y0, y1)
    return out.reshape(B, S, D)
```

```python
import functools

import jax
import jax.numpy as jnp
from jax import lax
from jax.experimental import pallas as pl
from jax.experimental.pallas import tpu as pltpu
from jax.experimental.pallas import tpu_sc as plsc

F32 = jnp.float32
BF16 = jnp.bfloat16
I32 = jnp.int32
U32 = jnp.uint32

D = 1024
B = 8
S = 4096
T = B * S
D_LRU = 512
D_SGU = 512
HEADS = 8
HEAD_DIM = 64
CONV_WIDTH = 4
LRU_C = 8.0
SGU_BLOCK = 128
CHUNK = 64
N_GROUPS = 4
EPG = 8
N_EXPERTS = N_GROUPS * EPG
TOP_K = 2
D_EXPERT = 512
EPS = 1e-6
DP = D // 2

LANES = 128
SUBLANES = 8
VMEM_LIMIT = 60 * 1024 * 1024

NH = 2
BH = B // NH
TH = BH * S

SC_CORES = 2
SC_SUBCORES = 16
SC_WORKERS = SC_CORES * SC_SUBCORES
SC_ROWS = 64
SC_TOK = TH // SC_WORKERS
SC_STEPS = SC_TOK // SC_ROWS

LT = SGU_BLOCK
ROWS = BH * LT
PITCH = LT + SUBLANES
HALO = SUBLANES
NSLAB = D_LRU // LANES

TM = 512
TMS = 256
N_SLOTS = TH * TOP_K
CAP = N_SLOTS + N_EXPERTS * TM
NBLK = CAP // TM
NBLK_PAD = LANES
TR = 1024
TKC = 512
LOGIT_W = LANES


_GELU_A = -2.0 * 0.7978845608028654 * 1.4426950408889634
_GELU_B = _GELU_A * 0.044715


def _gelu(x):
    return x / (1.0 + jnp.exp2(x * (_GELU_A + _GELU_B * (x * x))))


def _sigmoid(x):
    return 1.0 / (1.0 + jnp.exp(-x))


def _pack_rows(x):
    lo = lax.bitcast_convert_type(x[:, 0:DP].astype(BF16).astype(F32), U32)
    hi = lax.bitcast_convert_type(x[:, DP:D].astype(BF16).astype(F32), U32)
    word = hi | (lo >> 16)
    return lax.bitcast_convert_type(word, I32)


def _unpack_rows(w):
    u = lax.bitcast_convert_type(w, U32)
    lo = lax.bitcast_convert_type(u << 16, F32)
    hi = lax.bitcast_convert_type(u & jnp.uint32(0xFFFF0000), F32)
    return jnp.concatenate([lo, hi], axis=1)


def _mod_kernel(c_ref, w_ref, b_ref, mul_ref, add_ref, o_ref):
    c = c_ref[...]
    cond = c * _sigmoid(c)
    m = jnp.dot(cond, w_ref[...], preferred_element_type=F32, precision=lax.Precision.HIGHEST)
    m = m + b_ref[0]
    o_ref[0] = m * mul_ref[0] + add_ref[0]


def _mod_call(c, ada_w, ada_b, mul, add):
    return pl.pallas_call(
        _mod_kernel,
        grid=(6,),
        in_specs=[
            pl.BlockSpec((B, D), lambda k: (0, 0)),
            pl.BlockSpec((D, D), lambda k: (0, k)),
            pl.BlockSpec((1, 1, D), lambda k: (k, 0, 0)),
            pl.BlockSpec((1, 1, D), lambda k: (k, 0, 0)),
            pl.BlockSpec((1, 1, D), lambda k: (k, 0, 0)),
        ],
        out_specs=pl.BlockSpec((1, B, D), lambda k: (k, 0, 0)),
        out_shape=jax.ShapeDtypeStruct((6, B, D), F32),
        name="mod",
    )(c, ada_w, ada_b.reshape(6, 1, D), mul, add)


N_MIXER_IN = 18


def _mixer_kernel(*refs, half):
    (x_ref, mod_ref, win_ref, convw_ref, convb_ref, wg_ref, bg_ref, lam_ref,
     lng_ref, lnb_ref, sguw_ref, sgub_ref, wout_ref, wr_ref, br_ref,
     e1_ref, e3_ref, e2_ref) = refs[:N_MIXER_IN]
    (x1_ref, h2_ref, lg_ref, e1b_ref, e3b_ref, e2b_ref,
     h_scr, xr_scr, a_scr, u_scr, hc_scr, y_scr) = refs[-12:]
    t = pl.program_id(0)

    e1b_ref[...] = e1_ref[...].astype(BF16)
    e3b_ref[...] = e3_ref[...].astype(BF16)
    e2b_ref[...] = e2_ref[...].astype(BF16)

    @pl.when(t == 0)
    def _init():
        hc_scr[...] = jnp.zeros_like(hc_scr)
        xr_scr[:, 0:HALO, :] = jnp.zeros((BH, HALO, D_LRU), F32)

    def mod_row(k, j):
        b = half * BH + j
        return mod_ref[k, b:b + 1, :]

    lam = lam_ref[...]
    z = -lam
    softplus = jnp.maximum(z, 0.0) + jnp.log1p(jnp.exp(-jnp.abs(z)))
    clam = -LRU_C * softplus
    cw = convw_ref[...]
    cb = convb_ref[...]

    for j in range(BH):
        xb = x_ref[j]
        ms = jnp.mean(xb * xb, axis=-1, keepdims=True)
        hb = (xb * lax.rsqrt(ms + EPS)) * mod_row(1, j) + mod_row(0, j)
        h_scr[j * LT:(j + 1) * LT, :] = hb.astype(BF16)
    hg = h_scr[...]
    xr = jnp.dot(hg, win_ref[:, 0:D_LRU], preferred_element_type=F32)
    xcs = []
    for j in range(BH):
        xr_scr[j, HALO:HALO + LT, :] = xr[j * LT:(j + 1) * LT, :]
        acc = cb + cw[CONV_WIDTH - 1:CONV_WIDTH] * xr_scr[j, HALO:HALO + LT, :]
        for k in range(CONV_WIDTH - 1):
            off = HALO - (CONV_WIDTH - 1) + k
            acc = acc + cw[k:k + 1] * xr_scr[j, off:off + LT, :]
        xcs.append(acc)
        xr_scr[j, 0:HALO, :] = xr_scr[j, LT:LT + HALO, :]
    xc = jnp.concatenate(xcs, axis=0)
    gpre = jnp.dot(xc.astype(BF16), wg_ref[...], preferred_element_type=F32) + bg_ref[...]
    r = _sigmoid(gpre[:, 0:D_LRU])
    gi = _sigmoid(gpre[:, D_LRU:2 * D_LRU])
    a = jnp.exp(clam * r)
    uin = jnp.sqrt(1.0 - a * a) * (gi * xc)
    for j in range(BH):
        for s in range(NSLAB):
            a_scr[s, j * PITCH:j * PITCH + LT, :] = a[j * LT:(j + 1) * LT, s * LANES:(s + 1) * LANES]
            u_scr[s, j * PITCH:j * PITCH + LT, :] = uin[j * LT:(j + 1) * LT, s * LANES:(s + 1) * LANES]

    def scan_body(tt, carry):
        out = []
        for s in range(NSLAB):
            at = a_scr[s, pl.ds(tt, BH, stride=PITCH), :]
            ut = u_scr[s, pl.ds(tt, BH, stride=PITCH), :]
            hn = at * carry[s] + ut
            u_scr[s, pl.ds(tt, BH, stride=PITCH), :] = hn
            out.append(hn)
        return tuple(out)

    carry = lax.fori_loop(0, LT, scan_body, tuple(hc_scr[s] for s in range(NSLAB)), unroll=8)
    for s in range(NSLAB):
        hc_scr[s] = carry[s]

    gr = jnp.dot(hg, win_ref[:, D_LRU:2 * D_LRU], preferred_element_type=F32)
    gg = _gelu(gr)
    for j in range(BH):
        hl = jnp.concatenate([u_scr[s, j * PITCH:j * PITCH + LT, :] for s in range(NSLAB)], axis=1)
        y_scr[j * LT:(j + 1) * LT, 0:D_LRU] = (hl * gg[j * LT:(j + 1) * LT, :]).astype(BF16)

    su = jnp.dot(hg, win_ref[:, 2 * D_LRU:2 * D_LRU + D_SGU], preferred_element_type=F32)
    sv = jnp.dot(hg, win_ref[:, 2 * D_LRU + D_SGU:], preferred_element_type=F32)
    ug = _gelu(su)
    vg = _gelu(sv)
    mu = jnp.mean(vg, axis=-1, keepdims=True)
    vc = vg - mu
    var = jnp.mean(vc * vc, axis=-1, keepdims=True)
    vn = vc * lax.rsqrt(var + EPS) * lng_ref[...] + lnb_ref[...]
    vnt = [vn[j * LT:(j + 1) * LT, :].T for j in range(BH)]
    st = []
    for g in range(HEADS):
        lhs = jnp.concatenate([vnt[j][g * HEAD_DIM:(g + 1) * HEAD_DIM, :] for j in range(BH)], axis=0)
        st.append(jnp.dot(lhs.astype(BF16), sguw_ref[g], preferred_element_type=F32))
    for j in range(BH):
        stj = jnp.concatenate([st[g][j * HEAD_DIM:(j + 1) * HEAD_DIM, :] for g in range(HEADS)], axis=0)
        sj = stj.T + sgub_ref[...]
        y_scr[j * LT:(j + 1) * LT, D_LRU:] = (ug[j * LT:(j + 1) * LT, :] * sj).astype(BF16)

    mix = jnp.dot(y_scr[...], wout_ref[...], preferred_element_type=F32)
    for j in range(BH):
        x1 = x_ref[j] + mod_row(2, j) * mix[j * LT:(j + 1) * LT, :]
        x1_ref[j] = x1
        ms = jnp.mean(x1 * x1, axis=-1, keepdims=True)
        h2 = (x1 * lax.rsqrt(ms + EPS)) * mod_row(4, j) + mod_row(3, j)
        h2_ref[j] = _pack_rows(h2)
        h_scr[j * LT:(j + 1) * LT, :] = h2.astype(BF16)
    lg = jnp.dot(h_scr[...], wr_ref[...], preferred_element_type=F32) + br_ref[...]
    for j in range(BH):
        lg_ref[j] = lg[j * LT:(j + 1) * LT, :]


def _const_spec(shape):
    nd = len(shape)
    return pl.BlockSpec(shape, lambda t, _nd=nd: (0,) * _nd, pipeline_mode=pl.Buffered(1))


STEPS = S // LT
E_PARTS = NH * STEPS // N_EXPERTS
E_PER_HALF = N_EXPERTS // NH


def _mixer_call(half, anchor, prev_bf16, x, mod, win, convw, convb, wg, bg, lam, lng, lnb, sguw, sgub,
                wout, wr, br, e1, e3, e2):
    out_tile = lambda w: pl.BlockSpec((BH, LT, w), lambda t: (0, t, 0))
    e_map = lambda t: (half * E_PER_HALF + t // E_PARTS, t % E_PARTS, 0)
    e13_spec = pl.BlockSpec((1, D // E_PARTS, D_EXPERT), e_map)
    e2_spec = pl.BlockSpec((1, D_EXPERT // E_PARTS, D), e_map)
    extra_specs = [pl.BlockSpec(memory_space=pl.ANY)]
    extra_args = [anchor]
    aliases = {}
    if prev_bf16 is not None:
        for k, buf in enumerate(prev_bf16):
            aliases[N_MIXER_IN + 1 + k] = 3 + k
            extra_specs.append(pl.BlockSpec(memory_space=pl.ANY))
            extra_args.append(buf)
    return pl.pallas_call(
        functools.partial(_mixer_kernel, half=half),
        grid=(STEPS,),
        in_specs=[
            pl.BlockSpec((BH, LT, D), lambda t: (half, t, 0)),
            _const_spec((6, B, D)),
            _const_spec((D, 2 * D)),
            _const_spec((CONV_WIDTH, D_LRU)),
            _const_spec((1, D_LRU)),
            _const_spec((D_LRU, 2 * D_LRU)),
            _const_spec((1, 2 * D_LRU)),
            _const_spec((1, D_LRU)),
            _const_spec((1, D_SGU)),
            _const_spec((1, D_SGU)),
            _const_spec((HEADS, SGU_BLOCK, SGU_BLOCK)),
            _const_spec((SGU_BLOCK, D_SGU)),
            _const_spec((D, D)),
            _const_spec((D, LOGIT_W)),
            _const_spec((1, LOGIT_W)),
            e13_spec, e13_spec, e2_spec,
        ] + extra_specs,
        out_specs=[out_tile(D), out_tile(DP), out_tile(LOGIT_W), e13_spec, e13_spec, e2_spec],
        out_shape=[
            jax.ShapeDtypeStruct((BH, S, D), F32),
            jax.ShapeDtypeStruct((BH, S, DP), I32),
            jax.ShapeDtypeStruct((BH, S, LOGIT_W), F32),
            jax.ShapeDtypeStruct((N_EXPERTS, D, D_EXPERT), BF16),
            jax.ShapeDtypeStruct((N_EXPERTS, D, D_EXPERT), BF16),
            jax.ShapeDtypeStruct((N_EXPERTS, D_EXPERT, D), BF16),
        ],
        input_output_aliases=aliases,
        scratch_shapes=[
            pltpu.VMEM((ROWS, D), BF16),
            pltpu.VMEM((BH, LT + HALO, D_LRU), F32),
            pltpu.VMEM((NSLAB, BH * PITCH, LANES), F32),
            pltpu.VMEM((NSLAB, BH * PITCH, LANES), F32),
            pltpu.VMEM((NSLAB, BH, LANES), F32),
            pltpu.VMEM((ROWS, D), BF16),
        ],
        compiler_params=pltpu.CompilerParams(
            dimension_semantics=("arbitrary",), vmem_limit_bytes=VMEM_LIMIT),
        name="mixer",
    )(x, mod, win, convw, convb, wg, bg, lam, lng, lnb, sguw, sgub, wout, wr, br, e1, e3, e2, *extra_args)


def _route_kernel(lg_ref, dest_ref, wt_ref, be_ref, cnt_scr, pst_scr, run_scr):
    p = pl.program_id(0)
    i = pl.program_id(1)

    lt = lg_ref[...].T
    row8 = lax.broadcasted_iota(I32, (SUBLANES, TR), 0)
    gl = jnp.where(row8 < N_GROUPS, lt[0:SUBLANES], -jnp.inf)
    gmax = jnp.max(gl, axis=0, keepdims=True)
    gsel = jnp.min(jnp.where(gl == gmax, row8, SUBLANES), axis=0, keepdims=True)
    pg = 1.0 / jnp.sum(jnp.exp(gl - gmax), axis=0, keepdims=True)
    el = lt[SUBLANES:2 * SUBLANES]
    for g in range(1, N_GROUPS):
        el = jnp.where(gsel == g, lt[(g + 1) * SUBLANES:(g + 2) * SUBLANES], el)
    emax = jnp.max(el, axis=0, keepdims=True)
    pe = jnp.exp(el - emax)
    prob = pe / jnp.sum(pe, axis=0, keepdims=True)
    p0 = jnp.max(prob, axis=0, keepdims=True)
    i0 = jnp.min(jnp.where(prob == p0, row8, SUBLANES), axis=0, keepdims=True)
    pm = jnp.where(row8 == i0, -1.0, prob)
    p1 = jnp.max(pm, axis=0, keepdims=True)
    i1 = jnp.min(jnp.where(pm == p1, row8, SUBLANES), axis=0, keepdims=True)
    den = p0 + p1
    w0 = pg * (p0 / den)
    w1 = pg * (p1 / den)
    e0 = gsel * EPG + i0
    e1 = gsel * EPG + i1

    row32 = lax.broadcasted_iota(I32, (N_EXPERTS, TR), 0)
    is0 = row32 == e0
    is1 = row32 == e1
    oh = jnp.where(is0, 1.0, 0.0) + jnp.where(is1, 1.0, 0.0)
    ohsum = jnp.sum(oh, axis=1, keepdims=True)

    @pl.when((p == 0) & (i == 0))
    def _zero():
        cnt_scr[...] = jnp.zeros_like(cnt_scr)

    @pl.when(p == 0)
    def _count():
        cnt_scr[...] = cnt_scr[...] + ohsum

    @pl.when((p == 1) & (i == 0))
    def _starts():
        cnt = cnt_scr[...]
        ntile = jnp.ceil(cnt * (1.0 / TM))
        r = lax.broadcasted_iota(I32, (N_EXPERTS, N_EXPERTS), 0)
        c = lax.broadcasted_iota(I32, (N_EXPERTS, N_EXPERTS), 1)
        lower = jnp.where(c < r, 1.0, 0.0).astype(BF16)
        tstart = jnp.dot(lower, ntile.astype(BF16), preferred_element_type=F32)
        pst_scr[...] = tstart * TM
        run_scr[...] = jnp.zeros_like(run_scr)
        tend = tstart + ntile
        lane = lax.broadcasted_iota(I32, (N_EXPERTS, NBLK_PAD), 1).astype(F32)
        ts1 = tstart[:, 0:1]
        te1 = tend[:, 0:1]
        be = jnp.sum(jnp.where(te1 <= lane, 1.0, 0.0), axis=0, keepdims=True)
        be = jnp.minimum(be, N_EXPERTS - 1.0)
        nused = jnp.sum(ntile[:, 0:1], axis=0, keepdims=True)
        left = jnp.clip(cnt[:, 0:1] - (lane - ts1) * TM, 0.0, TM * 1.0)
        nvalid = jnp.sum(jnp.where((ts1 <= lane) & (lane < te1), left, 0.0), axis=0, keepdims=True)
        rowb = lax.broadcasted_iota(I32, (SUBLANES, NBLK_PAD), 0)
        be_ref[...] = jnp.where(rowb == 0, be, jnp.where(rowb == 1, nused, nvalid)).astype(I32)

    @pl.when(p == 1)
    def _dest():
        rr = lax.broadcasted_iota(I32, (TR, TR), 0)
        cc = lax.broadcasted_iota(I32, (TR, TR), 1)
        upper = jnp.where(rr < cc, 1.0, 0.0).astype(BF16)
        cum = jnp.dot(oh.astype(BF16), upper, preferred_element_type=F32)
        pos = cum + (pst_scr[:, 0:1] + run_scr[:, 0:1])
        d0 = jnp.sum(jnp.where(is0, pos, 0.0), axis=0, keepdims=True)
        d1 = jnp.sum(jnp.where(is1, pos, 0.0), axis=0, keepdims=True)
        run_scr[...] = run_scr[...] + ohsum
        dest_ref[...] = jnp.where(row8 == 0, d0, jnp.where(row8 == 1, d1, 0.0)).astype(I32)
        rowl = lax.broadcasted_iota(I32, (LANES, TR), 0)
        wt_ref[...] = jnp.where(rowl == 0, w0, jnp.where(rowl == 1, w1, 0.0)).T


def _route_call(logits):
    return pl.pallas_call(
        _route_kernel,
        grid=(2, TH // TR),
        in_specs=[pl.BlockSpec((TR, LOGIT_W), lambda p, i: (i, 0))],
        out_specs=[
            pl.BlockSpec((SUBLANES, TR), lambda p, i: (0, i * p)),
            pl.BlockSpec((TR, LANES), lambda p, i: (i * p, 0)),
            pl.BlockSpec((SUBLANES, NBLK_PAD), lambda p, i: (0, 0)),
        ],
        out_shape=[
            jax.ShapeDtypeStruct((SUBLANES, TH), I32),
            jax.ShapeDtypeStruct((TH, LANES), F32),
            jax.ShapeDtypeStruct((SUBLANES, NBLK_PAD), I32),
        ],
        scratch_shapes=[
            pltpu.VMEM((N_EXPERTS, LANES), F32),
            pltpu.VMEM((N_EXPERTS, LANES), F32),
            pltpu.VMEM((N_EXPERTS, LANES), F32),
        ],
        compiler_params=pltpu.CompilerParams(
            dimension_semantics=("arbitrary", "arbitrary"), vmem_limit_bytes=VMEM_LIMIT),
        name="route",
    )(logits)


def _sc_mesh():
    return plsc.VectorSubcoreMesh(core_axis_name="c", subcore_axis_name="s")


def _sc_worker():
    return lax.axis_index("s") * SC_CORES + lax.axis_index("c")


def _dispatch_kernel(h2_hbm, d0_hbm, d1_hbm, xs_hbm, idx0_v, idx1_v, rows_v, sem):
    wid = _sc_worker()
    pltpu.sync_copy(d0_hbm.at[wid], idx0_v)
    pltpu.sync_copy(d1_hbm.at[wid], idx1_v)
    base = wid * SC_TOK
    for j in range(SC_STEPS):
        start = pl.multiple_of(base + j * SC_ROWS, SC_ROWS)
        pltpu.sync_copy(h2_hbm.at[pl.ds(start, SC_ROWS)], rows_v)
        c0 = pltpu.async_copy(rows_v, xs_hbm.at[idx0_v.at[j]], sem)
        c1 = pltpu.async_copy(rows_v, xs_hbm.at[idx1_v.at[j]], sem)
        c0.wait()
        c1.wait()


def _dispatch_call(h2p, d0, d1):
    return pl.kernel(
        _dispatch_kernel,
        out_type=jax.ShapeDtypeStruct((CAP, DP), I32),
        mesh=_sc_mesh(),
        scratch_types=[
            pltpu.VMEM((SC_STEPS, SC_ROWS), I32),
            pltpu.VMEM((SC_STEPS, SC_ROWS), I32),
            pltpu.VMEM((SC_ROWS, DP), I32),
            pltpu.SemaphoreType.DMA,
        ],
        cost_estimate=pl.CostEstimate(flops=0, transcendentals=0, bytes_accessed=3 * TH * DP * 4),
        name="dispatch",
    )(h2p, d0, d1)


def _gather_kernel(ys_hbm, d0_hbm, d1_hbm, y0_hbm, y1_hbm, idx0_v, idx1_v, rows_v, sem):
    wid = _sc_worker()
    pltpu.sync_copy(d0_hbm.at[wid], idx0_v)
    pltpu.sync_copy(d1_hbm.at[wid], idx1_v)
    base = wid * SC_TOK
    for j in range(SC_STEPS):
        start = pl.multiple_of(base + j * SC_ROWS, SC_ROWS)
        pltpu.async_copy(ys_hbm.at[idx0_v.at[j]], rows_v, sem).wait()
        pltpu.sync_copy(rows_v, y0_hbm.at[pl.ds(start, SC_ROWS)])
        pltpu.async_copy(ys_hbm.at[idx1_v.at[j]], rows_v, sem).wait()
        pltpu.sync_copy(rows_v, y1_hbm.at[pl.ds(start, SC_ROWS)])


def _gather_call(ys, d0, d1):
    row = jax.ShapeDtypeStruct((TH, DP), I32)
    return pl.kernel(
        _gather_kernel,
        out_type=(row, row),
        mesh=_sc_mesh(),
        scratch_types=[
            pltpu.VMEM((SC_STEPS, SC_ROWS), I32),
            pltpu.VMEM((SC_STEPS, SC_ROWS), I32),
            pltpu.VMEM((SC_ROWS, DP), I32),
            pltpu.SemaphoreType.DMA,
        ],
        cost_estimate=pl.CostEstimate(flops=0, transcendentals=0, bytes_accessed=4 * TH * DP * 4),
        name="gather",
    )(ys, d0, d1)


def _experts_kernel(be_ref, nu_ref, nv_ref, xs_ref, w1_ref, w3_ref, w2_ref, ys_ref):
    j = pl.program_id(0)

    @pl.when(j < nu_ref[0])
    def _active():
        for q in range(TM // TMS):
            rows = slice(q * TMS, (q + 1) * TMS)
            row = lax.broadcasted_iota(I32, (TMS, DP), 0) + q * TMS
            words = jnp.where(row < nv_ref[j], xs_ref[rows, :], 0)
            xb = _unpack_rows(words).astype(BF16)
            h1 = jnp.dot(xb, w1_ref[0], preferred_element_type=F32)
            h3 = jnp.dot(xb, w3_ref[0], preferred_element_type=F32)
            act = (h1 * _sigmoid(h1)) * h3
            y = jnp.dot(act.astype(BF16), w2_ref[0], preferred_element_type=F32)
            ys_ref[rows, :] = _pack_rows(y)


def _experts_call(block_expert, n_used, n_valid, xs, w1, w3, w2):
    def x_map(j, be, nu, nv):
        return (jnp.minimum(j, nu[0] - 1), 0)

    def w_map(j, be, nu, nv):
        return (be[jnp.minimum(j, nu[0] - 1)], 0, 0)

    grid_spec = pltpu.PrefetchScalarGridSpec(
        num_scalar_prefetch=3,
        grid=(NBLK,),
        in_specs=[
            pl.BlockSpec((TM, DP), x_map),
            pl.BlockSpec((1, D, D_EXPERT), w_map),
            pl.BlockSpec((1, D, D_EXPERT), w_map),
            pl.BlockSpec((1, D_EXPERT, D), w_map),
        ],
        out_specs=pl.BlockSpec((TM, DP), x_map),
    )
    return pl.pallas_call(
        _experts_kernel,
        grid_spec=grid_spec,
        out_shape=jax.ShapeDtypeStruct((CAP, DP), I32),
        compiler_params=pltpu.CompilerParams(
            dimension_semantics=("arbitrary",), vmem_limit_bytes=VMEM_LIMIT),
        name="experts",
    )(block_expert, n_used, n_valid, xs, w1, w3, w2)


def _combine_kernel(x1_ref, g2_ref, wt_ref, fg_ref, y0_ref, y1_ref, *rest):
    o_ref = rest[-1]
    wt = wt_ref[...]
    y = wt[:, 0:1] * _unpack_rows(y0_ref[...]) + wt[:, 1:2] * _unpack_rows(y1_ref[...])
    x2 = x1_ref[...] + g2_ref[0] * y
    ms = jnp.mean(x2 * x2, axis=-1, keepdims=True)
    o_ref[...] = (x2 * lax.rsqrt(ms + EPS)) * fg_ref[...]


def _combine_call(half, prev_out, x1, mod, wt, final_g, y0, y1):
    per_b = S // TKC
    nsteps = TH // TKC
    in_specs = [
        pl.BlockSpec((TKC, D), lambda i: (i, 0)),
        pl.BlockSpec((None, 1, 1, D), lambda i: (5, half * BH + i // per_b, 0, 0)),
        pl.BlockSpec((TKC, LANES), lambda i: (i, 0)),
        pl.BlockSpec((1, D), lambda i: (0, 0)),
        pl.BlockSpec((TKC, DP), lambda i: (i, 0)),
        pl.BlockSpec((TKC, DP), lambda i: (i, 0)),
    ]
    args = [x1, mod.reshape(6, B, 1, D), wt, final_g, y0, y1]
    aliases = {}
    if prev_out is not None:
        in_specs.append(pl.BlockSpec(memory_space=pl.ANY))
        args.append(prev_out)
        aliases = {len(args) - 1: 0}
    return pl.pallas_call(
        _combine_kernel,
        grid=(nsteps,),
        in_specs=in_specs,
        out_specs=pl.BlockSpec((TKC, D), lambda i: (half * nsteps + i, 0)),
        out_shape=jax.ShapeDtypeStruct((T, D), F32),
        input_output_aliases=aliases,
        compiler_params=pltpu.CompilerParams(
            dimension_semantics=("arbitrary",), vmem_limit_bytes=VMEM_LIMIT),
        name="combine",
    )(*args)


def _block_diag(w):
    eye = jnp.eye(HEADS, dtype=w.dtype)
    return jnp.einsum("hde,hg->hdge", w, eye).reshape(D_LRU, D_LRU)


def kernel(x, c, ada_w, ada_b, norm1_g, w_in, conv_w, conv_b, gate_a_w, gate_a_b, gate_i_w, gate_i_b, lru_lambda, sgu_ln_g, sgu_ln_b, sgu_w, sgu_b, w_out, norm2_g, router_group_w, router_group_b, router_expert_w, router_expert_b, expert_w1, expert_w3, expert_w2, final_g):
    l = 0
    ones = jnp.ones((D,), F32)
    zeros = jnp.zeros((D,), F32)
    mul = jnp.stack([ones, norm1_g[l], ones, ones, norm2_g[l], ones]).reshape(6, 1, D)
    add = jnp.stack([zeros, norm1_g[l], zeros, zeros, norm2_g[l], zeros]).reshape(6, 1, D)
    mod = _mod_call(c, ada_w[l], ada_b[l], mul, add)

    pos_chunk = jnp.arange(SGU_BLOCK) // CHUNK
    mask = (pos_chunk[None, :] <= pos_chunk[:, None]).astype(F32)
    sguw_t = jnp.transpose(sgu_w[l] * mask[None], (0, 2, 1)).astype(BF16)
    sgub_tile = jnp.repeat(sgu_b[l].T, HEAD_DIM, axis=1)
    wg = jnp.concatenate([_block_diag(gate_a_w[l]), _block_diag(gate_i_w[l])], axis=1).astype(BF16)
    bg = jnp.concatenate([gate_a_b[l], gate_i_b[l]]).reshape(1, 2 * D_LRU)
    wr = jnp.zeros((D, LOGIT_W), F32)
    wr = wr.at[:, 0:N_GROUPS].set(router_group_w[l]).at[:, SUBLANES:SUBLANES + N_EXPERTS].set(router_expert_w[l])
    br = jnp.zeros((1, LOGIT_W), F32)
    br = br.at[0, 0:N_GROUPS].set(router_group_b[l]).at[0, SUBLANES:SUBLANES + N_EXPERTS].set(router_expert_b[l])
    mixer_params = (
        w_in[l].astype(BF16), conv_w[l], conv_b[l].reshape(1, D_LRU), wg, bg,
        lru_lambda[l].reshape(1, D_LRU), sgu_ln_g[l].reshape(1, D_SGU), sgu_ln_b[l].reshape(1, D_SGU),
        sguw_t, sgub_tile, w_out[l].astype(BF16), wr.astype(BF16), br)

    routed = []
    anchor = jnp.zeros((SUBLANES, NBLK_PAD), I32)
    ebf16 = None
    for half in range(NH):
        x1, h2p, logits, e1b, e3b, e2b = _mixer_call(
            half, anchor, ebf16, x, mod, *mixer_params, expert_w1[l], expert_w3[l], expert_w2[l])
        ebf16 = (e1b, e3b, e2b)
        dest, wt, be = _route_call(logits.reshape(TH, LOGIT_W))
        anchor = be
        d0 = dest[0].reshape(SC_WORKERS, SC_STEPS, SC_ROWS)
        d1 = dest[1].reshape(SC_WORKERS, SC_STEPS, SC_ROWS)
        xs = _dispatch_call(h2p.reshape(TH, DP), d0, d1)
        routed.append((x1.reshape(TH, D), wt, be, d0, d1, xs))

    staged = []
    for x1, wt, be, d0, d1, xs in routed:
        ys = _experts_call(be[0, 0:NBLK], be[1, 0:1], be[2, 0:NBLK], xs, *ebf16)
        y0, y1 = _gather_call(ys, d0, d1)
        staged.append((x1, wt, y0, y1))

    out = None
    for half, (x1, wt, y0, y1) in enumerate(staged):
        out = _combine_call(half, out, x1, mod, wt, final_g.reshape(1, D), y0, y1)
    return out.reshape(B, S, D)
```

```python
import functools

import jax
import jax.numpy as jnp
from jax import lax
from jax.experimental import pallas as pl
from jax.experimental.pallas import tpu as pltpu
from jax.experimental.pallas import tpu_sc as plsc

F32 = jnp.float32
BF16 = jnp.bfloat16
I32 = jnp.int32
U32 = jnp.uint32

D = 1024
B = 8
S = 4096
T = B * S
D_LRU = 512
D_SGU = 512
HEADS = 8
HEAD_DIM = 64
CONV_WIDTH = 4
LRU_C = 8.0
SGU_BLOCK = 128
CHUNK = 64
N_GROUPS = 4
EPG = 8
N_EXPERTS = N_GROUPS * EPG
TOP_K = 2
D_EXPERT = 512
EPS = 1e-6
DP = D // 2

LANES = 128
SUBLANES = 8
VMEM_LIMIT = 60 * 1024 * 1024

NH = 2
BH = B // NH
TH = BH * S

SC_CORES = 2
SC_SUBCORES = 16
SC_WORKERS = SC_CORES * SC_SUBCORES
SC_ROWS = 64
SC_TOK = TH // SC_WORKERS
SC_STEPS = SC_TOK // SC_ROWS

LT = SGU_BLOCK
ROWS = BH * LT
PITCH = LT + SUBLANES
HALO = SUBLANES
NSLAB = D_LRU // LANES

TM = 512
N_SLOTS = TH * TOP_K
CAP = N_SLOTS + N_EXPERTS * TM
NBLK = CAP // TM
NBLK_PAD = LANES
TR = 1024
TKC = 512
LOGIT_W = LANES


_GELU_A = -2.0 * 0.7978845608028654 * 1.4426950408889634
_GELU_B = _GELU_A * 0.044715


def _gelu(x):
    return x / (1.0 + jnp.exp2(x * (_GELU_A + _GELU_B * (x * x))))


def _sigmoid(x):
    return 1.0 / (1.0 + jnp.exp(-x))


def _pack_rows(x):
    lo = lax.bitcast_convert_type(x[:, 0:DP].astype(BF16).astype(F32), U32)
    hi = lax.bitcast_convert_type(x[:, DP:D].astype(BF16).astype(F32), U32)
    word = hi | (lo >> 16)
    return lax.bitcast_convert_type(word, I32)


def _unpack_rows(w):
    u = lax.bitcast_convert_type(w, U32)
    lo = lax.bitcast_convert_type(u << 16, F32)
    hi = lax.bitcast_convert_type(u & jnp.uint32(0xFFFF0000), F32)
    return jnp.concatenate([lo, hi], axis=1)


def _mod_kernel(c_ref, w_ref, b_ref, mul_ref, add_ref, o_ref):
    c = c_ref[...]
    cond = c * _sigmoid(c)
    m = jnp.dot(cond, w_ref[...], preferred_element_type=F32, precision=lax.Precision.HIGHEST)
    m = m + b_ref[0]
    o_ref[0] = m * mul_ref[0] + add_ref[0]


def _mod_call(c, ada_w, ada_b, mul, add):
    return pl.pallas_call(
        _mod_kernel,
        grid=(6,),
        in_specs=[
            pl.BlockSpec((B, D), lambda k: (0, 0)),
            pl.BlockSpec((D, D), lambda k: (0, k)),
            pl.BlockSpec((1, 1, D), lambda k: (k, 0, 0)),
            pl.BlockSpec((1, 1, D), lambda k: (k, 0, 0)),
            pl.BlockSpec((1, 1, D), lambda k: (k, 0, 0)),
        ],
        out_specs=pl.BlockSpec((1, B, D), lambda k: (k, 0, 0)),
        out_shape=jax.ShapeDtypeStruct((6, B, D), F32),
        name="mod",
    )(c, ada_w, ada_b.reshape(6, 1, D), mul, add)


N_MIXER_IN = 18


def _mixer_kernel(*refs, half):
    (x_ref, mod_ref, win_ref, convw_ref, convb_ref, wg_ref, bg_ref, lam_ref,
     lng_ref, lnb_ref, sguw_ref, sgub_ref, wout_ref, wr_ref, br_ref,
     e1_ref, e3_ref, e2_ref) = refs[:N_MIXER_IN]
    (x1_ref, h2_ref, lg_ref, e1b_ref, e3b_ref, e2b_ref,
     h_scr, xr_scr, a_scr, u_scr, hc_scr, y_scr) = refs[-12:]
    t = pl.program_id(0)

    e1b_ref[...] = e1_ref[...].astype(BF16)
    e3b_ref[...] = e3_ref[...].astype(BF16)
    e2b_ref[...] = e2_ref[...].astype(BF16)

    @pl.when(t == 0)
    def _init():
        hc_scr[...] = jnp.zeros_like(hc_scr)
        xr_scr[:, 0:HALO, :] = jnp.zeros((BH, HALO, D_LRU), F32)

    def mod_row(k, j):
        b = half * BH + j
        return mod_ref[k, b:b + 1, :]

    lam = lam_ref[...]
    z = -lam
    softplus = jnp.maximum(z, 0.0) + jnp.log1p(jnp.exp(-jnp.abs(z)))
    clam = -LRU_C * softplus
    cw = convw_ref[...]
    cb = convb_ref[...]

    for j in range(BH):
        xb = x_ref[j]
        ms = jnp.mean(xb * xb, axis=-1, keepdims=True)
        hb = (xb * lax.rsqrt(ms + EPS)) * mod_row(1, j) + mod_row(0, j)
        h_scr[j * LT:(j + 1) * LT, :] = hb.astype(BF16)
    hg = h_scr[...]
    xr = jnp.dot(hg, win_ref[:, 0:D_LRU], preferred_element_type=F32)
    xcs = []
    for j in range(BH):
        xr_scr[j, HALO:HALO + LT, :] = xr[j * LT:(j + 1) * LT, :]
        acc = cb + cw[CONV_WIDTH - 1:CONV_WIDTH] * xr_scr[j, HALO:HALO + LT, :]
        for k in range(CONV_WIDTH - 1):
            off = HALO - (CONV_WIDTH - 1) + k
            acc = acc + cw[k:k + 1] * xr_scr[j, off:off + LT, :]
        xcs.append(acc)
        xr_scr[j, 0:HALO, :] = xr_scr[j, LT:LT + HALO, :]
    xc = jnp.concatenate(xcs, axis=0)
    gpre = jnp.dot(xc.astype(BF16), wg_ref[...], preferred_element_type=F32) + bg_ref[...]
    r = _sigmoid(gpre[:, 0:D_LRU])
    gi = _sigmoid(gpre[:, D_LRU:2 * D_LRU])
    a = jnp.exp(clam * r)
    uin = jnp.sqrt(1.0 - a * a) * (gi * xc)
    for j in range(BH):
        for s in range(NSLAB):
            a_scr[s, j * PITCH:j * PITCH + LT, :] = a[j * LT:(j + 1) * LT, s * LANES:(s + 1) * LANES]
            u_scr[s, j * PITCH:j * PITCH + LT, :] = uin[j * LT:(j + 1) * LT, s * LANES:(s + 1) * LANES]

    def scan_body(tt, carry):
        out = []
        for s in range(NSLAB):
            at = a_scr[s, pl.ds(tt, BH, stride=PITCH), :]
            ut = u_scr[s, pl.ds(tt, BH, stride=PITCH), :]
            hn = at * carry[s] + ut
            u_scr[s, pl.ds(tt, BH, stride=PITCH), :] = hn
            out.append(hn)
        return tuple(out)

    carry = lax.fori_loop(0, LT, scan_body, tuple(hc_scr[s] for s in range(NSLAB)), unroll=8)
    for s in range(NSLAB):
        hc_scr[s] = carry[s]

    gr = jnp.dot(hg, win_ref[:, D_LRU:2 * D_LRU], preferred_element_type=F32)
    gg = _gelu(gr)
    for j in range(BH):
        hl = jnp.concatenate([u_scr[s, j * PITCH:j * PITCH + LT, :] for s in range(NSLAB)], axis=1)
        y_scr[j * LT:(j + 1) * LT, 0:D_LRU] = (hl * gg[j * LT:(j + 1) * LT, :]).astype(BF16)

    su = jnp.dot(hg, win_ref[:, 2 * D_LRU:2 * D_LRU + D_SGU], preferred_element_type=F32)
    sv = jnp.dot(hg, win_ref[:, 2 * D_LRU + D_SGU:], preferred_element_type=F32)
    ug = _gelu(su)
    vg = _gelu(sv)
    mu = jnp.mean(vg, axis=-1, keepdims=True)
    vc = vg - mu
    var = jnp.mean(vc * vc, axis=-1, keepdims=True)
    vn = vc * lax.rsqrt(var + EPS) * lng_ref[...] + lnb_ref[...]
    vnt = [vn[j * LT:(j + 1) * LT, :].T for j in range(BH)]
    st = []
    for g in range(HEADS):
        lhs = jnp.concatenate([vnt[j][g * HEAD_DIM:(g + 1) * HEAD_DIM, :] for j in range(BH)], axis=0)
        st.append(jnp.dot(lhs.astype(BF16), sguw_ref[g], preferred_element_type=F32))
    for j in range(BH):
        stj = jnp.concatenate([st[g][j * HEAD_DIM:(j + 1) * HEAD_DIM, :] for g in range(HEADS)], axis=0)
        sj = stj.T + sgub_ref[...]
        y_scr[j * LT:(j + 1) * LT, D_LRU:] = (ug[j * LT:(j + 1) * LT, :] * sj).astype(BF16)

    mix = jnp.dot(y_scr[...], wout_ref[...], preferred_element_type=F32)
    for j in range(BH):
        x1 = x_ref[j] + mod_row(2, j) * mix[j * LT:(j + 1) * LT, :]
        x1_ref[j] = x1
        ms = jnp.mean(x1 * x1, axis=-1, keepdims=True)
        h2 = (x1 * lax.rsqrt(ms + EPS)) * mod_row(4, j) + mod_row(3, j)
        h2_ref[j] = _pack_rows(h2)
        h_scr[j * LT:(j + 1) * LT, :] = h2.astype(BF16)
    lg = jnp.dot(h_scr[...], wr_ref[...], preferred_element_type=F32) + br_ref[...]
    for j in range(BH):
        lg_ref[j] = lg[j * LT:(j + 1) * LT, :]


def _const_spec(shape):
    nd = len(shape)
    return pl.BlockSpec(shape, lambda t, _nd=nd: (0,) * _nd, pipeline_mode=pl.Buffered(1))


STEPS = S // LT
E_PARTS = NH * STEPS // N_EXPERTS
E_PER_HALF = N_EXPERTS // NH


def _mixer_call(half, anchors, prev_bf16, x, mod, win, convw, convb, wg, bg, lam, lng, lnb, sguw, sgub,
                wout, wr, br, e1, e3, e2):
    out_tile = lambda w: pl.BlockSpec((BH, LT, w), lambda t: (0, t, 0))
    e_map = lambda t: (half * E_PER_HALF + t // E_PARTS, t % E_PARTS, 0)
    e13_spec = pl.BlockSpec((1, D // E_PARTS, D_EXPERT), e_map)
    e2_spec = pl.BlockSpec((1, D_EXPERT // E_PARTS, D), e_map)
    extra_specs = [pl.BlockSpec(memory_space=pl.ANY) for _ in anchors]
    extra_args = list(anchors)
    aliases = {}
    if prev_bf16 is not None:
        for k, buf in enumerate(prev_bf16):
            aliases[N_MIXER_IN + len(anchors) + k] = 3 + k
            extra_specs.append(pl.BlockSpec(memory_space=pl.ANY))
            extra_args.append(buf)
    return pl.pallas_call(
        functools.partial(_mixer_kernel, half=half),
        grid=(STEPS,),
        in_specs=[
            pl.BlockSpec((BH, LT, D), lambda t: (half, t, 0)),
            _const_spec((6, B, D)),
            _const_spec((D, 2 * D)),
            _const_spec((CONV_WIDTH, D_LRU)),
            _const_spec((1, D_LRU)),
            _const_spec((D_LRU, 2 * D_LRU)),
            _const_spec((1, 2 * D_LRU)),
            _const_spec((1, D_LRU)),
            _const_spec((1, D_SGU)),
            _const_spec((1, D_SGU)),
            _const_spec((HEADS, SGU_BLOCK, SGU_BLOCK)),
            _const_spec((SGU_BLOCK, D_SGU)),
            _const_spec((D, D)),
            _const_spec((D, LOGIT_W)),
            _const_spec((1, LOGIT_W)),
            e13_spec, e13_spec, e2_spec,
        ] + extra_specs,
        out_specs=[out_tile(D), out_tile(DP), out_tile(LOGIT_W), e13_spec, e13_spec, e2_spec],
        out_shape=[
            jax.ShapeDtypeStruct((BH, S, D), F32),
            jax.ShapeDtypeStruct((BH, S, DP), I32),
            jax.ShapeDtypeStruct((BH, S, LOGIT_W), F32),
            jax.ShapeDtypeStruct((N_EXPERTS, D, D_EXPERT), BF16),
            jax.ShapeDtypeStruct((N_EXPERTS, D, D_EXPERT), BF16),
            jax.ShapeDtypeStruct((N_EXPERTS, D_EXPERT, D), BF16),
        ],
        input_output_aliases=aliases,
        scratch_shapes=[
            pltpu.VMEM((ROWS, D), BF16),
            pltpu.VMEM((BH, LT + HALO, D_LRU), F32),
            pltpu.VMEM((NSLAB, BH * PITCH, LANES), F32),
            pltpu.VMEM((NSLAB, BH * PITCH, LANES), F32),
            pltpu.VMEM((NSLAB, BH, LANES), F32),
            pltpu.VMEM((ROWS, D), BF16),
        ],
        compiler_params=pltpu.CompilerParams(
            dimension_semantics=("arbitrary",), vmem_limit_bytes=VMEM_LIMIT),
        cost_estimate=pl.CostEstimate(
            flops=2 * TH * (D * 2 * D + D_LRU * 2 * D_LRU + D_SGU * SGU_BLOCK + D * D + D * LOGIT_W),
            transcendentals=TH * (4 * D_LRU + 2 * 3 * D_SGU),
            bytes_accessed=TH * (2 * D + DP + LOGIT_W) * 4 + 3 * D * D_EXPERT * (N_EXPERTS // NH) * 6),
        name="mixer",
    )(x, mod, win, convw, convb, wg, bg, lam, lng, lnb, sguw, sgub, wout, wr, br, e1, e3, e2, *extra_args)


def _route_kernel(lg_ref, dest_ref, wt_ref, be_ref, cnt_scr, pst_scr, run_scr):
    p = pl.program_id(0)
    i = pl.program_id(1)

    lt = lg_ref[...].T
    row8 = lax.broadcasted_iota(I32, (SUBLANES, TR), 0)
    gl = jnp.where(row8 < N_GROUPS, lt[0:SUBLANES], -jnp.inf)
    gmax = jnp.max(gl, axis=0, keepdims=True)
    gsel = jnp.min(jnp.where(gl == gmax, row8, SUBLANES), axis=0, keepdims=True)
    pg = 1.0 / jnp.sum(jnp.exp(gl - gmax), axis=0, keepdims=True)
    el = lt[SUBLANES:2 * SUBLANES]
    for g in range(1, N_GROUPS):
        el = jnp.where(gsel == g, lt[(g + 1) * SUBLANES:(g + 2) * SUBLANES], el)
    emax = jnp.max(el, axis=0, keepdims=True)
    pe = jnp.exp(el - emax)
    prob = pe / jnp.sum(pe, axis=0, keepdims=True)
    p0 = jnp.max(prob, axis=0, keepdims=True)
    i0 = jnp.min(jnp.where(prob == p0, row8, SUBLANES), axis=0, keepdims=True)
    pm = jnp.where(row8 == i0, -1.0, prob)
    p1 = jnp.max(pm, axis=0, keepdims=True)
    i1 = jnp.min(jnp.where(pm == p1, row8, SUBLANES), axis=0, keepdims=True)
    den = p0 + p1
    w0 = pg * (p0 / den)
    w1 = pg * (p1 / den)
    e0 = gsel * EPG + i0
    e1 = gsel * EPG + i1

    row32 = lax.broadcasted_iota(I32, (N_EXPERTS, TR), 0)
    is0 = row32 == e0
    is1 = row32 == e1
    oh = jnp.where(is0, 1.0, 0.0) + jnp.where(is1, 1.0, 0.0)
    ohsum = jnp.sum(oh, axis=1, keepdims=True)

    @pl.when((p == 0) & (i == 0))
    def _zero():
        cnt_scr[...] = jnp.zeros_like(cnt_scr)

    @pl.when(p == 0)
    def _count():
        cnt_scr[...] = cnt_scr[...] + ohsum

    @pl.when((p == 1) & (i == 0))
    def _starts():
        cnt = cnt_scr[...]
        ntile = jnp.ceil(cnt * (1.0 / TM))
        r = lax.broadcasted_iota(I32, (N_EXPERTS, N_EXPERTS), 0)
        c = lax.broadcasted_iota(I32, (N_EXPERTS, N_EXPERTS), 1)
        lower = jnp.where(c < r, 1.0, 0.0).astype(BF16)
        tstart = jnp.dot(lower, ntile.astype(BF16), preferred_element_type=F32)
        pst_scr[...] = tstart * TM
        run_scr[...] = jnp.zeros_like(run_scr)
        tend = tstart + ntile
        lane = lax.broadcasted_iota(I32, (N_EXPERTS, NBLK_PAD), 1).astype(F32)
        ts1 = tstart[:, 0:1]
        te1 = tend[:, 0:1]
        be = jnp.sum(jnp.where(te1 <= lane, 1.0, 0.0), axis=0, keepdims=True)
        be = jnp.minimum(be, N_EXPERTS - 1.0)
        nused = jnp.sum(ntile[:, 0:1], axis=0, keepdims=True)
        left = jnp.clip(cnt[:, 0:1] - (lane - ts1) * TM, 0.0, TM * 1.0)
        nvalid = jnp.sum(jnp.where((ts1 <= lane) & (lane < te1), left, 0.0), axis=0, keepdims=True)
        rowb = lax.broadcasted_iota(I32, (SUBLANES, NBLK_PAD), 0)
        be_ref[...] = jnp.where(rowb == 0, be, jnp.where(rowb == 1, nused, nvalid)).astype(I32)

    @pl.when(p == 1)
    def _dest():
        rr = lax.broadcasted_iota(I32, (TR, TR), 0)
        cc = lax.broadcasted_iota(I32, (TR, TR), 1)
        upper = jnp.where(rr < cc, 1.0, 0.0).astype(BF16)
        cum = jnp.dot(oh.astype(BF16), upper, preferred_element_type=F32)
        pos = cum + (pst_scr[:, 0:1] + run_scr[:, 0:1])
        d0 = jnp.sum(jnp.where(is0, pos, 0.0), axis=0, keepdims=True)
        d1 = jnp.sum(jnp.where(is1, pos, 0.0), axis=0, keepdims=True)
        run_scr[...] = run_scr[...] + ohsum
        dest_ref[...] = jnp.where(row8 == 0, d0, jnp.where(row8 == 1, d1, 0.0)).astype(I32)
        rowl = lax.broadcasted_iota(I32, (LANES, TR), 0)
        wt_ref[...] = jnp.where(rowl == 0, w0, jnp.where(rowl == 1, w1, 0.0)).T


def _route_call(logits):
    return pl.pallas_call(
        _route_kernel,
        grid=(2, TH // TR),
        in_specs=[pl.BlockSpec((TR, LOGIT_W), lambda p, i: (i, 0))],
        out_specs=[
            pl.BlockSpec((SUBLANES, TR), lambda p, i: (0, i * p)),
            pl.BlockSpec((TR, LANES), lambda p, i: (i * p, 0)),
            pl.BlockSpec((SUBLANES, NBLK_PAD), lambda p, i: (0, 0)),
        ],
        out_shape=[
            jax.ShapeDtypeStruct((SUBLANES, TH), I32),
            jax.ShapeDtypeStruct((TH, LANES), F32),
            jax.ShapeDtypeStruct((SUBLANES, NBLK_PAD), I32),
        ],
        scratch_shapes=[
            pltpu.VMEM((N_EXPERTS, LANES), F32),
            pltpu.VMEM((N_EXPERTS, LANES), F32),
            pltpu.VMEM((N_EXPERTS, LANES), F32),
        ],
        compiler_params=pltpu.CompilerParams(
            dimension_semantics=("arbitrary", "arbitrary"), vmem_limit_bytes=VMEM_LIMIT),
        cost_estimate=pl.CostEstimate(
            flops=2 * N_EXPERTS * TR * TH, transcendentals=2 * 16 * TH,
            bytes_accessed=TH * (2 * LOGIT_W + SUBLANES + LANES) * 4),
        name="route",
    )(logits)


def _sc_mesh():
    return plsc.VectorSubcoreMesh(core_axis_name="c", subcore_axis_name="s")


def _sc_worker():
    return lax.axis_index("s") * SC_CORES + lax.axis_index("c")


def _dispatch_kernel(h2_hbm, d0_hbm, d1_hbm, xs_hbm, idx0_v, idx1_v, rows_v, sem):
    wid = _sc_worker()
    pltpu.sync_copy(d0_hbm.at[wid], idx0_v)
    pltpu.sync_copy(d1_hbm.at[wid], idx1_v)
    base = wid * SC_TOK
    for j in range(SC_STEPS):
        start = pl.multiple_of(base + j * SC_ROWS, SC_ROWS)
        pltpu.sync_copy(h2_hbm.at[pl.ds(start, SC_ROWS)], rows_v)
        c0 = pltpu.async_copy(rows_v, xs_hbm.at[idx0_v.at[j]], sem)
        c1 = pltpu.async_copy(rows_v, xs_hbm.at[idx1_v.at[j]], sem)
        c0.wait()
        c1.wait()


def _dispatch_call(h2p, d0, d1):
    return pl.kernel(
        _dispatch_kernel,
        out_type=jax.ShapeDtypeStruct((CAP, DP), I32),
        mesh=_sc_mesh(),
        scratch_types=[
            pltpu.VMEM((SC_STEPS, SC_ROWS), I32),
            pltpu.VMEM((SC_STEPS, SC_ROWS), I32),
            pltpu.VMEM((SC_ROWS, DP), I32),
            pltpu.SemaphoreType.DMA,
        ],
        cost_estimate=pl.CostEstimate(flops=0, transcendentals=0, bytes_accessed=3 * TH * DP * 4),
        name="dispatch",
    )(h2p, d0, d1)


def _gather_kernel(ys_hbm, d0_hbm, d1_hbm, y0_hbm, y1_hbm, idx0_v, idx1_v, rows_v, sem):
    wid = _sc_worker()
    pltpu.sync_copy(d0_hbm.at[wid], idx0_v)
    pltpu.sync_copy(d1_hbm.at[wid], idx1_v)
    base = wid * SC_TOK
    for j in range(SC_STEPS):
        start = pl.multiple_of(base + j * SC_ROWS, SC_ROWS)
        pltpu.async_copy(ys_hbm.at[idx0_v.at[j]], rows_v, sem).wait()
        pltpu.sync_copy(rows_v, y0_hbm.at[pl.ds(start, SC_ROWS)])
        pltpu.async_copy(ys_hbm.at[idx1_v.at[j]], rows_v, sem).wait()
        pltpu.sync_copy(rows_v, y1_hbm.at[pl.ds(start, SC_ROWS)])


def _gather_call(ys, d0, d1):
    row = jax.ShapeDtypeStruct((TH, DP), I32)
    return pl.kernel(
        _gather_kernel,
        out_type=(row, row),
        mesh=_sc_mesh(),
        scratch_types=[
            pltpu.VMEM((SC_STEPS, SC_ROWS), I32),
            pltpu.VMEM((SC_STEPS, SC_ROWS), I32),
            pltpu.VMEM((SC_ROWS, DP), I32),
            pltpu.SemaphoreType.DMA,
        ],
        cost_estimate=pl.CostEstimate(flops=0, transcendentals=0, bytes_accessed=4 * TH * DP * 4),
        name="gather",
    )(ys, d0, d1)


def _experts_kernel(be_ref, nu_ref, nv_ref, xs_ref, w1_ref, w3_ref, w2_ref, ys_ref):
    j = pl.program_id(0)

    @pl.when(j < nu_ref[0])
    def _active():
        row = lax.broadcasted_iota(I32, (TM, DP), 0)
        words = jnp.where(row < nv_ref[j], xs_ref[...], 0)
        xb = _unpack_rows(words).astype(BF16)
        h1 = jnp.dot(xb, w1_ref[0], preferred_element_type=F32)
        h3 = jnp.dot(xb, w3_ref[0], preferred_element_type=F32)
        act = (h1 * _sigmoid(h1)) * h3
        y = jnp.dot(act.astype(BF16), w2_ref[0], preferred_element_type=F32)
        ys_ref[...] = _pack_rows(y)


def _experts_call(block_expert, n_used, n_valid, xs, w1, w3, w2):
    def x_map(j, be, nu, nv):
        return (jnp.minimum(j, nu[0] - 1), 0)

    def w_map(j, be, nu, nv):
        return (be[jnp.minimum(j, nu[0] - 1)], 0, 0)

    grid_spec = pltpu.PrefetchScalarGridSpec(
        num_scalar_prefetch=3,
        grid=(NBLK,),
        in_specs=[
            pl.BlockSpec((TM, DP), x_map),
            pl.BlockSpec((1, D, D_EXPERT), w_map),
            pl.BlockSpec((1, D, D_EXPERT), w_map),
            pl.BlockSpec((1, D_EXPERT, D), w_map),
        ],
        out_specs=pl.BlockSpec((TM, DP), x_map),
    )
    return pl.pallas_call(
        _experts_kernel,
        grid_spec=grid_spec,
        out_shape=jax.ShapeDtypeStruct((CAP, DP), I32),
        compiler_params=pltpu.CompilerParams(
            dimension_semantics=("arbitrary",), vmem_limit_bytes=VMEM_LIMIT),
        cost_estimate=pl.CostEstimate(
            flops=2 * N_SLOTS * 3 * D * D_EXPERT,
            transcendentals=N_SLOTS * D_EXPERT,
            bytes_accessed=2 * N_SLOTS * DP * 4 + 3 * N_EXPERTS * D * D_EXPERT * 2),
        name="experts",
    )(block_expert, n_used, n_valid, xs, w1, w3, w2)


def _combine_kernel(x1_ref, g2_ref, wt_ref, fg_ref, y0_ref, y1_ref, *rest):
    o_ref = rest[-1]
    wt = wt_ref[...]
    y = wt[:, 0:1] * _unpack_rows(y0_ref[...]) + wt[:, 1:2] * _unpack_rows(y1_ref[...])
    x2 = x1_ref[...] + g2_ref[0] * y
    ms = jnp.mean(x2 * x2, axis=-1, keepdims=True)
    o_ref[...] = (x2 * lax.rsqrt(ms + EPS)) * fg_ref[...]


def _combine_call(half, prev_out, x1, mod, wt, final_g, y0, y1):
    per_b = S // TKC
    nsteps = TH // TKC
    in_specs = [
        pl.BlockSpec((TKC, D), lambda i: (i, 0)),
        pl.BlockSpec((None, 1, 1, D), lambda i: (5, half * BH + i // per_b, 0, 0)),
        pl.BlockSpec((TKC, LANES), lambda i: (i, 0)),
        pl.BlockSpec((1, D), lambda i: (0, 0)),
        pl.BlockSpec((TKC, DP), lambda i: (i, 0)),
        pl.BlockSpec((TKC, DP), lambda i: (i, 0)),
    ]
    args = [x1, mod.reshape(6, B, 1, D), wt, final_g, y0, y1]
    aliases = {}
    if prev_out is not None:
        in_specs.append(pl.BlockSpec(memory_space=pl.ANY))
        args.append(prev_out)
        aliases = {len(args) - 1: 0}
    return pl.pallas_call(
        _combine_kernel,
        grid=(nsteps,),
        in_specs=in_specs,
        out_specs=pl.BlockSpec((TKC, D), lambda i: (half * nsteps + i, 0)),
        out_shape=jax.ShapeDtypeStruct((T, D), F32),
        input_output_aliases=aliases,
        compiler_params=pltpu.CompilerParams(
            dimension_semantics=("arbitrary",), vmem_limit_bytes=VMEM_LIMIT),
        cost_estimate=pl.CostEstimate(
            flops=10 * TH * D, transcendentals=TH,
            bytes_accessed=TH * (2 * D + 2 * DP + LANES) * 4),
        name="combine",
    )(*args)


def _block_diag(w):
    eye = jnp.eye(HEADS, dtype=w.dtype)
    return jnp.einsum("hde,hg->hdge", w, eye).reshape(D_LRU, D_LRU)


def kernel(x, c, ada_w, ada_b, norm1_g, w_in, conv_w, conv_b, gate_a_w, gate_a_b, gate_i_w, gate_i_b, lru_lambda, sgu_ln_g, sgu_ln_b, sgu_w, sgu_b, w_out, norm2_g, router_group_w, router_group_b, router_expert_w, router_expert_b, expert_w1, expert_w3, expert_w2, final_g):
    l = 0
    ones = jnp.ones((D,), F32)
    zeros = jnp.zeros((D,), F32)
    mul = jnp.stack([ones, norm1_g[l], ones, ones, norm2_g[l], ones]).reshape(6, 1, D)
    add = jnp.stack([zeros, norm1_g[l], zeros, zeros, norm2_g[l], zeros]).reshape(6, 1, D)
    mod = _mod_call(c, ada_w[l], ada_b[l], mul, add)

    pos_chunk = jnp.arange(SGU_BLOCK) // CHUNK
    mask = (pos_chunk[None, :] <= pos_chunk[:, None]).astype(F32)
    sguw_t = jnp.transpose(sgu_w[l] * mask[None], (0, 2, 1)).astype(BF16)
    sgub_tile = jnp.repeat(sgu_b[l].T, HEAD_DIM, axis=1)
    wg = jnp.concatenate([_block_diag(gate_a_w[l]), _block_diag(gate_i_w[l])], axis=1).astype(BF16)
    bg = jnp.concatenate([gate_a_b[l], gate_i_b[l]]).reshape(1, 2 * D_LRU)
    wr = jnp.zeros((D, LOGIT_W), F32)
    wr = wr.at[:, 0:N_GROUPS].set(router_group_w[l]).at[:, SUBLANES:SUBLANES + N_EXPERTS].set(router_expert_w[l])
    br = jnp.zeros((1, LOGIT_W), F32)
    br = br.at[0, 0:N_GROUPS].set(router_group_b[l]).at[0, SUBLANES:SUBLANES + N_EXPERTS].set(router_expert_b[l])
    mixer_params = (
        w_in[l].astype(BF16), conv_w[l], conv_b[l].reshape(1, D_LRU), wg, bg,
        lru_lambda[l].reshape(1, D_LRU), sgu_ln_g[l].reshape(1, D_SGU), sgu_ln_b[l].reshape(1, D_SGU),
        sguw_t, sgub_tile, w_out[l].astype(BF16), wr.astype(BF16), br)

    routed = []
    anchors = (jnp.zeros((SC_WORKERS, SC_STEPS, SC_ROWS), I32),) * 2
    ebf16 = None
    for half in range(NH):
        x1, h2p, logits, e1b, e3b, e2b = _mixer_call(
            half, anchors, ebf16, x, mod, *mixer_params, expert_w1[l], expert_w3[l], expert_w2[l])
        ebf16 = (e1b, e3b, e2b)
        dest, wt, be = _route_call(logits.reshape(TH, LOGIT_W))
        d0 = dest[0].reshape(SC_WORKERS, SC_STEPS, SC_ROWS)
        d1 = dest[1].reshape(SC_WORKERS, SC_STEPS, SC_ROWS)
        anchors = (d0, d1)
        xs = _dispatch_call(h2p.reshape(TH, DP), d0, d1)
        routed.append((x1.reshape(TH, D), wt, be, d0, d1, xs))

    staged = []
    for x1, wt, be, d0, d1, xs in routed:
        ys = _experts_call(be[0, 0:NBLK], be[1, 0:1], be[2, 0:NBLK], xs, *ebf16)
        y0, y1 = _gather_call(ys, d0, d1)
        staged.append((x1, wt, y0, y1))

    out = None
    for half, (x1, wt, y0, y1) in enumerate(staged):
        out = _combine_call(half, out, x1, mod, wt, final_g.reshape(1, D), y0, y1)
    return out.reshape(B, S, D)
```
